```python
import jax
import jax.numpy as jnp
from jax import lax
import numpy as np

D_MODEL = 2048
BATCH = 4
SEQ = 4096
DEPTH = 1

HEAD_DIM = 128
MOBA_HEADS = D_MODEL // (2 * HEAD_DIM)
RET_HEADS = D_MODEL // (2 * HEAD_DIM)
MOBA_WIDTH = MOBA_HEADS * HEAD_DIM
RET_WIDTH = RET_HEADS * HEAD_DIM
MIX_WIDTH = MOBA_WIDTH + RET_WIDTH
MOBA_BLOCK = 256
MOBA_TOPK = 3
MOBA_QCHUNK = 64
RET_CHUNK = 256
ROPE_THETA = 10000.0
D_FF = 256 * ((8 * D_MODEL // 3 + 255) // 256)
CONV_WIDTH = 3
NORM_EPS = 1e-6
GN_EPS = 1e-5
IN_SPLITS = (MOBA_WIDTH, 2 * MOBA_WIDTH, 3 * MOBA_WIDTH,
             3 * MOBA_WIDTH + RET_WIDTH, 3 * MOBA_WIDTH + 2 * RET_WIDTH,
             3 * MOBA_WIDTH + 3 * RET_WIDTH)
IN_COLS = 3 * MOBA_WIDTH + 4 * RET_WIDTH

kernel_name = "hymba_moba_retnet_convffn_layer"


def rms_norm(x, w):
    xf = x.astype(jnp.float32)
    y = xf * lax.rsqrt(jnp.mean(xf * xf, axis=-1, keepdims=True) + NORM_EPS)
    return (y * w.astype(jnp.float32)).astype(x.dtype)


def rope_tables(seq, dim, dtype):
    pos = jnp.arange(seq, dtype=jnp.float32)
    inv = 1.0 / (ROPE_THETA ** (jnp.arange(0, dim, 2, dtype=jnp.float32) / dim))
    ang = pos[:, None] * inv[None, :]
    return jnp.cos(ang).astype(dtype), jnp.sin(ang).astype(dtype)


def apply_rope(t, cos, sin):
    t1, t2 = jnp.split(t, 2, axis=-1)
    return jnp.concatenate([t1 * cos - t2 * sin, t1 * sin + t2 * cos], axis=-1)


def pad_seq(t, s_pad):
    return jnp.pad(t, ((0, 0), (0, 0), (0, s_pad - t.shape[2]), (0, 0)))


def moba_attention(q, k, v):
    b, h, s, d = q.shape
    nb = -(-s // MOBA_BLOCK)
    s_pad = nb * MOBA_BLOCK
    q, k, v = pad_seq(q, s_pad), pad_seq(k, s_pad), pad_seq(v, s_pad)
    scale = d ** -0.5
    k_blocks = k.reshape(b, h, nb, MOBA_BLOCK, d)
    v_blocks = v.reshape(b, h, nb, MOBA_BLOCK, d)
    k_mean = jnp.mean(k_blocks.astype(jnp.float32), axis=3).astype(q.dtype)
    gate = jnp.einsum('bhsd,bhnd->bhsn', q, k_mean, preferred_element_type=jnp.float32)
    q_blk = jnp.arange(s_pad) // MOBA_BLOCK
    past = jnp.arange(nb)[None, :] < q_blk[:, None]
    gate = jnp.where(past, gate, -jnp.inf)
    n_sel = min(MOBA_TOPK, nb)
    _, sel = lax.top_k(gate, n_sel)

    nc = s_pad // MOBA_QCHUNK
    q_chunks = jnp.moveaxis(q.reshape(b, h, nc, MOBA_QCHUNK, d), 2, 0)
    sel_chunks = jnp.moveaxis(sel.reshape(b, h, nc, MOBA_QCHUNK, n_sel), 2, 0)
    gather = jax.vmap(jax.vmap(lambda blocks, idx: blocks[idx]))

    def chunk(args):
        c, q_c, sel_c = args
        own = (c * MOBA_QCHUNK) // MOBA_BLOCK
        qpos = c * MOBA_QCHUNK + jnp.arange(MOBA_QCHUNK)
        kpos = own * MOBA_BLOCK + jnp.arange(MOBA_BLOCK)
        k_own = lax.dynamic_index_in_dim(k_blocks, own, axis=2, keepdims=False)
        v_own = lax.dynamic_index_in_dim(v_blocks, own, axis=2, keepdims=False)
        s_own = jnp.einsum('bhqd,bhkd->bhqk', q_c, k_own,
                           preferred_element_type=jnp.float32) * scale
        scores = [jnp.where(kpos[None, :] <= qpos[:, None], s_own, -jnp.inf)]
        for slot in range(n_sel):
            k_g = gather(k_blocks, sel_c[..., slot])
            s_g = jnp.einsum('bhqd,bhqkd->bhqk', q_c, k_g,
                             preferred_element_type=jnp.float32) * scale
            scores.append(jnp.where(slot < own, s_g, -jnp.inf))
        p = jax.nn.softmax(jnp.concatenate(scores, axis=-1), axis=-1).astype(v.dtype)
        p = p.reshape(b, h, MOBA_QCHUNK, n_sel + 1, MOBA_BLOCK)
        out = jnp.einsum('bhqk,bhkd->bhqd', p[:, :, :, 0], v_own,
                         preferred_element_type=jnp.float32)
        for slot in range(n_sel):
            v_g = gather(v_blocks, sel_c[..., slot])
            out = out + jnp.einsum('bhqk,bhqkd->bhqd', p[:, :, :, slot + 1], v_g,
                                   preferred_element_type=jnp.float32)
        return out.astype(v.dtype)

    out = lax.map(chunk, (jnp.arange(nc), q_chunks, sel_chunks))
    out = jnp.moveaxis(out, 0, 2).reshape(b, h, s_pad, d)
    return out[:, :, :s]


def retention(q, k, v):
    b, h, s, d = q.shape
    nc = -(-s // RET_CHUNK)
    s_pad = nc * RET_CHUNK
    q, k, v = pad_seq(q, s_pad), pad_seq(k, s_pad), pad_seq(v, s_pad)
    k = k * (d ** -0.5)
    log_g = jnp.log(1.0 - 2.0 ** (-5.0 - jnp.arange(h, dtype=jnp.float32)))
    idx = jnp.arange(RET_CHUNK, dtype=jnp.float32)
    rel = idx[:, None] - idx[None, :]
    decay_in = jnp.where(rel >= 0, jnp.exp(log_g[:, None, None] * jnp.maximum(rel, 0.0)), 0.0)
    xi = jnp.exp(log_g[:, None] * (idx + 1.0))
    zeta = jnp.exp(log_g[:, None] * (RET_CHUNK - 1.0 - idx))
    g_chunk = jnp.exp(log_g * RET_CHUNK)
    qc = q.reshape(b, h, nc, RET_CHUNK, d)
    kc = k.reshape(b, h, nc, RET_CHUNK, d)
    vc = v.reshape(b, h, nc, RET_CHUNK, d)
    inner = jnp.einsum('bhnqd,bhnkd->bhnqk', qc, kc,
                       preferred_element_type=jnp.float32) * decay_in[None, :, None]
    y_in = jnp.einsum('bhnqk,bhnkd->bhnqd', inner.astype(vc.dtype), vc,
                      preferred_element_type=jnp.float32)
    kv = jnp.einsum('bhnkd,bhnke->bhnde', kc.astype(jnp.float32) * zeta[None, :, None, :, None],
                    vc.astype(jnp.float32))

    def step(state, kv_n):
        return state * g_chunk[None, :, None, None] + kv_n, state

    _, prev = lax.scan(step, jnp.zeros((b, h, d, d), jnp.float32), jnp.moveaxis(kv, 2, 0))
    prev = jnp.moveaxis(prev, 0, 2)
    y_cross = jnp.einsum('bhnqd,bhnde->bhnqe', qc.astype(jnp.float32), prev) * xi[None, :, None, :, None]
    y = (y_in + y_cross).reshape(b, h, s_pad, d)
    return y[:, :, :s]


def hybrid_mixer(hn, w_in, ret_gn_w, ret_gn_b, w_out):
    b, s, _ = hn.shape
    proj = hn @ w_in
    q_a, k_a, v_a, q_r, k_r, v_r, g_r = jnp.split(proj, IN_SPLITS, axis=-1)

    def heads(t, n):
        return t.reshape(b, s, n, HEAD_DIM).transpose(0, 2, 1, 3)

    cos, sin = rope_tables(s, HEAD_DIM, hn.dtype)
    a = moba_attention(apply_rope(heads(q_a, MOBA_HEADS), cos, sin),
                       apply_rope(heads(k_a, MOBA_HEADS), cos, sin),
                       heads(v_a, MOBA_HEADS))
    a = a.transpose(0, 2, 1, 3).reshape(b, s, MOBA_WIDTH)
    r = retention(apply_rope(heads(q_r, RET_HEADS), cos, sin),
                  apply_rope(heads(k_r, RET_HEADS), cos, sin),
                  heads(v_r, RET_HEADS))
    mu = jnp.mean(r, axis=-1, keepdims=True)
    var = jnp.mean(jnp.square(r - mu), axis=-1, keepdims=True)
    r = ((r - mu) * lax.rsqrt(var + GN_EPS)).transpose(0, 2, 1, 3).reshape(b, s, RET_WIDTH)
    r = r * ret_gn_w.astype(jnp.float32) + ret_gn_b.astype(jnp.float32)
    r = (jax.nn.silu(g_r.astype(jnp.float32)) * r).astype(hn.dtype)
    return jnp.concatenate([a, r], axis=-1) @ w_out


def conv_glu_ffn(hn, w_up, conv_w, conv_b, w_down):
    u = hn @ w_up
    u_g, u_v = jnp.split(u, 2, axis=-1)
    u_g = lax.conv_general_dilated(
        u_g, conv_w[:, None, :].astype(u_g.dtype), window_strides=(1,),
        padding=[(CONV_WIDTH - 1, 0)], dimension_numbers=('NWC', 'WIO', 'NWC'),
        feature_group_count=D_FF) + conv_b
    return (jax.nn.silu(u_g) * u_v) @ w_down


def setup_inputs(seed: int = 0) -> dict:
    key = jax.random.key(seed)
    ks = jax.random.split(key, 13)
    f32 = jnp.float32

    def nrm(k, shape, scale):
        return jax.random.normal(k, shape, f32) * scale

    return {
        "x": nrm(ks[0], (BATCH, SEQ, D_MODEL), 1.0),
        "norm_mix_pre": 1.0 + nrm(ks[1], (DEPTH, D_MODEL), 0.05),
        "w_in": nrm(ks[2], (DEPTH, D_MODEL, IN_COLS), D_MODEL ** -0.5),
        "ret_gn_w": 1.0 + nrm(ks[3], (DEPTH, RET_WIDTH), 0.05),
        "ret_gn_b": nrm(ks[4], (DEPTH, RET_WIDTH), 0.02),
        "w_out": nrm(ks[5], (DEPTH, MIX_WIDTH, D_MODEL), MIX_WIDTH ** -0.5),
        "norm_mix_post": 1.0 + nrm(ks[6], (DEPTH, D_MODEL), 0.05),
        "norm_ffn_pre": 1.0 + nrm(ks[7], (DEPTH, D_MODEL), 0.05),
        "w_up": nrm(ks[8], (DEPTH, D_MODEL, 2 * D_FF), D_MODEL ** -0.5),
        "conv_w": nrm(ks[9], (DEPTH, CONV_WIDTH, D_FF), CONV_WIDTH ** -0.5),
        "conv_b": nrm(ks[10], (DEPTH, D_FF), 0.02),
        "w_down": nrm(ks[11], (DEPTH, D_FF, D_MODEL), D_FF ** -0.5),
        "norm_ffn_post": 1.0 + nrm(ks[12], (DEPTH, D_MODEL), 0.05),
    }


def reference(x, norm_mix_pre, w_in, ret_gn_w, ret_gn_b, w_out, norm_mix_post,
              norm_ffn_pre, w_up, conv_w, conv_b, w_down, norm_ffn_post):
    for layer in range(DEPTH):
        hn = rms_norm(x, norm_mix_pre[layer])
        mix = hybrid_mixer(hn, w_in[layer], ret_gn_w[layer], ret_gn_b[layer], w_out[layer])
        x = x + rms_norm(mix, norm_mix_post[layer])
        hn = rms_norm(x, norm_ffn_pre[layer])
        ff = conv_glu_ffn(hn, w_up[layer], conv_w[layer], conv_b[layer], w_down[layer])
        x = x + rms_norm(ff, norm_ffn_post[layer])
    return x
```

```python
import functools

import jax
import jax.numpy as jnp
from jax import lax
from jax.experimental import pallas as pl
from jax.experimental.pallas import tpu as pltpu

HEAD_DIM = 128
N_HEADS = 8
GROUP_WIDTH = N_HEADS * HEAD_DIM
MOBA_BLOCK = 256
MOBA_TOPK = 3
RET_CHUNK = 256
ROPE_THETA = 10000.0
CONV_WIDTH = 3
NORM_EPS = 1e-6
GN_EPS = 1e-5
ATTN_SCALE = HEAD_DIM ** -0.5
MASKED_SCORE = -1e30

V7X_VMEM_LIMIT_BYTES = 56 * 1024 * 1024

F32 = jnp.float32
BF16 = jnp.bfloat16


def _dot(a, b):
    return jnp.dot(a, b, preferred_element_type=F32)


def _dot_nt(a, b):
    return lax.dot_general(a, b, (((1,), (1,)), ((), ())), preferred_element_type=F32)


def _rms_scale(v):
    return lax.rsqrt(jnp.mean(v * v, axis=-1, keepdims=True) + NORM_EPS)


def _inproj_kernel(x_ref, nw_ref, w_ref, cos_ref, sin_ref, o_ref, hn_ref):
    j = pl.program_id(1)

    @pl.when(j == 0)
    def _():
        x = x_ref[...]
        hn_ref[...] = (x * _rms_scale(x) * nw_ref[...]).astype(BF16)

    acc = _dot(hn_ref[...], w_ref[...])
    is_rope = (j == 0) | (j == 1) | (j == 3) | (j == 4)

    @pl.when(is_rope)
    def _():
        cos = cos_ref[...]
        sin = sin_ref[...]
        for h in range(N_HEADS):
            t = acc[:, h * HEAD_DIM:(h + 1) * HEAD_DIM]
            rot = pltpu.roll(t, HEAD_DIM // 2, axis=1)
            o_ref[:, h * HEAD_DIM:(h + 1) * HEAD_DIM] = (t * cos + rot * sin).astype(BF16)

    @pl.when(jnp.logical_not(is_rope))
    def _():
        o_ref[...] = acc.astype(BF16)


def _inproj(x2, norm_w, w_in_bf16, cos_full, sin_signed, seq, tm):
    tokens, d_model = x2.shape
    in_cols = w_in_bf16.shape[1]
    n_groups = in_cols // GROUP_WIDTH
    tiles_per_seq = seq // tm
    return pl.pallas_call(
        _inproj_kernel,
        grid=(tokens // tm, n_groups),
        in_specs=[
            pl.BlockSpec((tm, d_model), lambda i, j: (i, 0)),
            pl.BlockSpec((1, d_model), lambda i, j: (0, 0)),
            pl.BlockSpec((d_model, GROUP_WIDTH), lambda i, j: (0, j)),
            pl.BlockSpec((tm, HEAD_DIM), lambda i, j: (i % tiles_per_seq, 0)),
            pl.BlockSpec((tm, HEAD_DIM), lambda i, j: (i % tiles_per_seq, 0)),
        ],
        out_specs=pl.BlockSpec((tm, GROUP_WIDTH), lambda i, j: (i, j)),
        out_shape=jax.ShapeDtypeStruct((tokens, in_cols), BF16),
        scratch_shapes=[pltpu.VMEM((tm, d_model), BF16)],
        compiler_params=pltpu.CompilerParams(
            dimension_semantics=("arbitrary", "arbitrary"),
            vmem_limit_bytes=V7X_VMEM_LIMIT_BYTES),
        name="inproj",
    )(x2, norm_w, w_in_bf16, cos_full, sin_signed)


def _moba_kernel(q_ref, k_ref, v_ref, o_ref, vt_ref, kmean_ref, bias_ref):
    seq = q_ref.shape[0]
    nb = seq // MOBA_BLOCK

    for jb in range(nb):
        rows = slice(jb * MOBA_BLOCK, (jb + 1) * MOBA_BLOCK)
        vt_ref[jb] = v_ref[rows, :].astype(F32).T.astype(BF16)
        kmean_ref[jb:jb + 1, :] = jnp.mean(k_ref[rows, :].astype(F32), axis=0, keepdims=True)

    blk_idx = lax.broadcasted_iota(jnp.int32, (nb, MOBA_BLOCK), 0)
    key_pos = lax.broadcasted_iota(jnp.int32, (MOBA_BLOCK, MOBA_BLOCK), 0)
    qry_pos = lax.broadcasted_iota(jnp.int32, (MOBA_BLOCK, MOBA_BLOCK), 1)
    causal = key_pos <= qry_pos

    def query_block(i, _):
        q0 = pl.multiple_of(i * MOBA_BLOCK, MOBA_BLOCK)
        q = q_ref[pl.ds(q0, MOBA_BLOCK), :]

        gate = _dot_nt(kmean_ref[...].astype(BF16), q)
        past = blk_idx < i
        gate = jnp.where(past, gate, -jnp.inf)
        beaten_by = jnp.zeros((nb, MOBA_BLOCK), jnp.int32)
        for jp in range(nb):
            row = gate[jp:jp + 1, :]
            beats = (row > gate) | ((row == gate) & (jp < blk_idx))
            beaten_by = beaten_by + beats.astype(jnp.int32)
        selected = past & (beaten_by < MOBA_TOPK)
        bias_ref[...] = jnp.where(selected, 0.0, MASKED_SCORE)

        k_own = k_ref[pl.ds(q0, MOBA_BLOCK), :]
        s = _dot_nt(k_own, q) * ATTN_SCALE
        s = jnp.where(causal, s, MASKED_SCORE)
        m = jnp.max(s, axis=0, keepdims=True)
        p = jnp.exp(s - m)
        l = jnp.sum(p, axis=0, keepdims=True)
        acc = _dot(vt_ref[i], p.astype(BF16))

        def key_block(j, carry):
            m, l, acc = carry
            k0 = pl.multiple_of(j * MOBA_BLOCK, MOBA_BLOCK)
            s = _dot_nt(k_ref[pl.ds(k0, MOBA_BLOCK), :], q) * ATTN_SCALE
            s = s + bias_ref[pl.ds(j, 1), :]
            m_new = jnp.maximum(m, jnp.max(s, axis=0, keepdims=True))
            alpha = jnp.exp(m - m_new)
            p = jnp.exp(s - m_new)
            l = alpha * l + jnp.sum(p, axis=0, keepdims=True)
            acc = alpha * acc + _dot(vt_ref[j], p.astype(BF16))
            return m_new, l, acc

        m, l, acc = lax.fori_loop(0, i, key_block, (m, l, acc))
        out_t = acc / l
        o_ref[pl.ds(q0, MOBA_BLOCK), :] = out_t.T.astype(BF16)
        return 0

    lax.fori_loop(0, nb, query_block, 0)


def _moba(proj3):
    batch, seq, _ = proj3.shape
    nb = seq // MOBA_BLOCK

    def col(group):
        return pl.BlockSpec((None, seq, HEAD_DIM), lambda b, h: (b, 0, group * N_HEADS + h))

    return pl.pallas_call(
        _moba_kernel,
        grid=(batch, N_HEADS),
        in_specs=[col(0), col(1), col(2)],
        out_specs=pl.BlockSpec((None, seq, HEAD_DIM), lambda b, h: (b, 0, h)),
        out_shape=jax.ShapeDtypeStruct((batch, seq, GROUP_WIDTH), BF16),
        scratch_shapes=[
            pltpu.VMEM((nb, HEAD_DIM, MOBA_BLOCK), BF16),
            pltpu.VMEM((nb, HEAD_DIM), F32),
            pltpu.VMEM((nb, MOBA_BLOCK), F32),
        ],
        compiler_params=pltpu.CompilerParams(
            dimension_semantics=("arbitrary", "arbitrary"),
            vmem_limit_bytes=V7X_VMEM_LIMIT_BYTES),
        name="moba",
    )(proj3, proj3, proj3)


def _retention_kernel(logg_ref, q_ref, k_ref, v_ref, g_ref, gw_ref, gb_ref, o_ref, state_ref):
    seq = q_ref.shape[0]
    nc = seq // RET_CHUNK
    log_g = logg_ref[pl.program_id(1)]

    row = lax.broadcasted_iota(jnp.int32, (RET_CHUNK, RET_CHUNK), 0)
    colm = lax.broadcasted_iota(jnp.int32, (RET_CHUNK, RET_CHUNK), 1)
    rel = (row - colm).astype(F32)
    decay = jnp.where(rel >= 0, jnp.exp(log_g * jnp.maximum(rel, 0.0)), 0.0) * ATTN_SCALE
    pos = lax.broadcasted_iota(jnp.int32, (RET_CHUNK, HEAD_DIM), 0).astype(F32)
    xi = jnp.exp(log_g * (pos + 1.0))
    zeta = jnp.exp(log_g * (RET_CHUNK - 1.0 - pos)) * ATTN_SCALE
    g_chunk = jnp.exp(log_g * RET_CHUNK)
    gn_w = gw_ref[...]
    gn_b = gb_ref[...]

    state_ref[...] = jnp.zeros_like(state_ref)

    def chunk(c, _):
        r0 = pl.multiple_of(c * RET_CHUNK, RET_CHUNK)
        rows = pl.ds(r0, RET_CHUNK)
        qc = q_ref[rows, :]
        kc = k_ref[rows, :]
        vc = v_ref[rows, :]
        state = state_ref[...]
        inner = _dot_nt(qc, kc) * decay
        y = _dot(inner.astype(BF16), vc) + _dot(qc, state.astype(BF16)) * xi
        kz_t = (kc.astype(F32) * zeta).T.astype(BF16)
        state_ref[...] = state * g_chunk + _dot(kz_t, vc)
        mu = jnp.mean(y, axis=-1, keepdims=True)
        yc = y - mu
        var = jnp.mean(yc * yc, axis=-1, keepdims=True)
        r = yc * lax.rsqrt(var + GN_EPS) * gn_w + gn_b
        g = g_ref[rows, :].astype(F32)
        o_ref[rows, :] = (g * jax.nn.sigmoid(g) * r).astype(BF16)
        return 0

    lax.fori_loop(0, nc, chunk, 0)


def _retention(proj3, log_g, gn_w, gn_b):
    batch, seq, _ = proj3.shape

    def col(group):
        return pl.BlockSpec((None, seq, HEAD_DIM), lambda b, h, lg: (b, 0, group * N_HEADS + h))

    head_vec = pl.BlockSpec((1, HEAD_DIM), lambda b, h, lg: (0, h))
    return pl.pallas_call(
        _retention_kernel,
        grid_spec=pltpu.PrefetchScalarGridSpec(
            num_scalar_prefetch=1,
            grid=(batch, N_HEADS),
            in_specs=[col(3), col(4), col(5), col(6), head_vec, head_vec],
            out_specs=pl.BlockSpec((None, seq, HEAD_DIM), lambda b, h, lg: (b, 0, h)),
            scratch_shapes=[pltpu.VMEM((HEAD_DIM, HEAD_DIM), F32)],
        ),
        out_shape=jax.ShapeDtypeStruct((batch, seq, GROUP_WIDTH), BF16),
        compiler_params=pltpu.CompilerParams(
            dimension_semantics=("arbitrary", "arbitrary"),
            vmem_limit_bytes=V7X_VMEM_LIMIT_BYTES),
        name="retention",
    )(log_g, proj3, proj3, proj3, proj3, gn_w, gn_b)


def _outproj_kernel(a_ref, r_ref, w_ref, x_ref, post_ref, pre_ref, x1_ref, hn_ref):
    mix = _dot(a_ref[...], w_ref[:GROUP_WIDTH, :]) + _dot(r_ref[...], w_ref[GROUP_WIDTH:, :])
    x1 = x_ref[...] + mix * _rms_scale(mix) * post_ref[...]
    x1_ref[...] = x1
    hn_ref[...] = (x1 * _rms_scale(x1) * pre_ref[...]).astype(BF16)


def _outproj(a2, r2, w_out_bf16, x2, norm_post, norm_pre, tm):
    tokens, d_model = x2.shape
    row_tile = lambda width: pl.BlockSpec((tm, width), lambda i: (i, 0))
    whole = lambda shape: pl.BlockSpec(shape, lambda i: (0, 0))
    return pl.pallas_call(
        _outproj_kernel,
        grid=(tokens // tm,),
        in_specs=[row_tile(GROUP_WIDTH), row_tile(GROUP_WIDTH), whole(w_out_bf16.shape),
                  row_tile(d_model), whole((1, d_model)), whole((1, d_model))],
        out_specs=[row_tile(d_model), row_tile(d_model)],
        out_shape=[jax.ShapeDtypeStruct((tokens, d_model), F32),
                   jax.ShapeDtypeStruct((tokens, d_model), BF16)],
        compiler_params=pltpu.CompilerParams(
            dimension_semantics=("arbitrary",),
            vmem_limit_bytes=V7X_VMEM_LIMIT_BYTES),
        name="outproj",
    )(a2, r2, w_out_bf16, x2, norm_post, norm_pre)


def _ffn_kernel(hn_ref, wg_ref, wv_ref, cw_ref, cb_ref, wd_ref, x1_ref, post_ref,
                o_ref, acc_ref, carry_ref, *, tiles_per_seq):
    i = pl.program_id(0)
    j = pl.program_id(1)
    tm = hn_ref.shape[0]
    hn = hn_ref[...]
    ug = _dot(hn, wg_ref[...])
    uv = _dot(hn, wv_ref[...])

    @pl.when(i % tiles_per_seq == 0)
    def _():
        carry_ref[j] = jnp.zeros(carry_ref.shape[1:], F32)

    prev = carry_ref[j]
    carry_ref[j] = ug[tm - 8:, :]
    row = lax.broadcasted_iota(jnp.int32, ug.shape, 0)
    back1 = jnp.where(row == 0, prev[7:8, :], pltpu.roll(ug, 1, axis=0))
    back2 = jnp.where(row == 0, prev[6:7, :],
                      jnp.where(row == 1, prev[7:8, :], pltpu.roll(ug, 2, axis=0)))
    cw = cw_ref[...]
    conv = cw[0:1, :] * back2 + cw[1:2, :] * back1 + cw[2:3, :] * ug + cb_ref[...]
    hidden = (conv * jax.nn.sigmoid(conv) * uv).astype(BF16)
    part = _dot(hidden, wd_ref[...])

    @pl.when(j == 0)
    def _():
        acc_ref[...] = part

    @pl.when(j > 0)
    def _():
        acc_ref[...] += part

    @pl.when(j == pl.num_programs(1) - 1)
    def _():
        ff = acc_ref[...]
        o_ref[...] = x1_ref[...] + ff * _rms_scale(ff) * post_ref[...]


def _ffn(hn2, w_up_bf16, conv_w, conv_b, w_down_bf16, x1, norm_post, seq, tm, tf):
    tokens, d_model = x1.shape
    d_ff = w_down_bf16.shape[0]
    n_chunks = d_ff // tf
    row_tile = pl.BlockSpec((tm, d_model), lambda i, j: (i, 0))
    return pl.pallas_call(
        functools.partial(_ffn_kernel, tiles_per_seq=seq // tm),
        grid=(tokens // tm, n_chunks),
        in_specs=[
            row_tile,
            pl.BlockSpec((d_model, tf), lambda i, j: (0, j)),
            pl.BlockSpec((d_model, tf), lambda i, j: (0, n_chunks + j)),
            pl.BlockSpec((CONV_WIDTH, tf), lambda i, j: (0, j)),
            pl.BlockSpec((1, tf), lambda i, j: (0, j)),
            pl.BlockSpec((tf, d_model), lambda i, j: (j, 0)),
            row_tile,
            pl.BlockSpec((1, d_model), lambda i, j: (0, 0)),
        ],
        out_specs=row_tile,
        out_shape=jax.ShapeDtypeStruct((tokens, d_model), F32),
        scratch_shapes=[pltpu.VMEM((tm, d_model), F32),
                        pltpu.VMEM((n_chunks, 8, tf), F32)],
        compiler_params=pltpu.CompilerParams(
            dimension_semantics=("arbitrary", "arbitrary"),
            vmem_limit_bytes=V7X_VMEM_LIMIT_BYTES),
        name="ffn",
    )(hn2, w_up_bf16, w_up_bf16, conv_w, conv_b, w_down_bf16, x1, norm_post)


def _rope_tables(seq):
    pos = jnp.arange(seq, dtype=F32)
    inv = 1.0 / (ROPE_THETA ** (jnp.arange(0, HEAD_DIM, 2, dtype=F32) / HEAD_DIM))
    ang = pos[:, None] * inv[None, :]
    cos, sin = jnp.cos(ang), jnp.sin(ang)
    return jnp.concatenate([cos, cos], axis=-1), jnp.concatenate([-sin, sin], axis=-1)


def _tile(n, preferred):
    for t in preferred:
        if n % t == 0:
            return t
    raise ValueError(f"no supported tile for extent {n}")


def kernel(x, norm_mix_pre, w_in, ret_gn_w, ret_gn_b, w_out, norm_mix_post, norm_ffn_pre, w_up, conv_w, conv_b, w_down, norm_ffn_post):
    batch, seq, d_model = x.shape
    depth = w_in.shape[0]
    assert seq % MOBA_BLOCK == 0 and seq % RET_CHUNK == 0
    assert w_in.shape[2] == 7 * GROUP_WIDTH and w_out.shape[1] == 2 * GROUP_WIDTH
    d_ff = w_down.shape[1]
    tokens = batch * seq
    tm_proj = _tile(seq, (1024, 512, 256))
    tm_row = _tile(seq, (512, 256))
    tf = _tile(d_ff, (512, 256, 128))

    cos_full, sin_signed = _rope_tables(seq)
    log_g = jnp.log(1.0 - 2.0 ** (-5.0 - jnp.arange(N_HEADS, dtype=F32)))

    x2 = x.reshape(tokens, d_model)
    for layer in range(depth):
        proj = _inproj(x2, norm_mix_pre[layer][None, :], w_in[layer].astype(BF16),
                       cos_full, sin_signed, seq, tm_proj)
        proj3 = proj.reshape(batch, seq, -1)
        a = _moba(proj3)
        r = _retention(proj3, log_g, ret_gn_w[layer][None, :], ret_gn_b[layer][None, :])
        x1, hn2 = _outproj(a.reshape(tokens, GROUP_WIDTH), r.reshape(tokens, GROUP_WIDTH),
                           w_out[layer].astype(BF16), x2,
                           norm_mix_post[layer][None, :], norm_ffn_pre[layer][None, :], tm_row)
        x2 = _ffn(hn2, w_up[layer].astype(BF16), conv_w[layer], conv_b[layer][None, :],
                  w_down[layer].astype(BF16), x1, norm_ffn_post[layer][None, :], seq, tm_row, tf)
    return x2.reshape(batch, seq, d_model)
```

```python
import functools

import jax
import jax.numpy as jnp
from jax import lax
from jax.experimental import pallas as pl
from jax.experimental.pallas import tpu as pltpu

HEAD_DIM = 128
N_HEADS = 8
GROUP_WIDTH = N_HEADS * HEAD_DIM
MOBA_BLOCK = 256
MOBA_TOPK = 3
RET_CHUNK = 256
ROPE_THETA = 10000.0
CONV_WIDTH = 3
NORM_EPS = 1e-6
GN_EPS = 1e-5
ATTN_SCALE = HEAD_DIM ** -0.5
MASKED_SCORE = -1e30
MOBA_Q_SCALE = ATTN_SCALE * 1.4426950408889634

V7X_VMEM_LIMIT_BYTES = 56 * 1024 * 1024

F32 = jnp.float32
BF16 = jnp.bfloat16


def _dot(a, b):
    return jnp.dot(a, b, preferred_element_type=F32)


def _dot_nt(a, b):
    return lax.dot_general(a, b, (((1,), (1,)), ((), ())), preferred_element_type=F32)


def _rms_scale(v):
    return lax.rsqrt(jnp.mean(v * v, axis=-1, keepdims=True) + NORM_EPS)


def _inproj_kernel(x_ref, nw_ref, w_ref, cos_ref, sin_ref, o_ref, hn_ref):
    j = pl.program_id(1)

    @pl.when(j == 0)
    def _():
        x = x_ref[...]
        hn_ref[...] = (x * _rms_scale(x) * nw_ref[...]).astype(BF16)

    acc = _dot(hn_ref[...], w_ref[...])
    is_rope = (j == 0) | (j == 1) | (j == 3) | (j == 4)

    def rope(out_scale):
        cos = cos_ref[...]
        sin = sin_ref[...]
        if out_scale is not None:
            cos, sin = cos * out_scale, sin * out_scale
        for h in range(N_HEADS):
            t = acc[:, h * HEAD_DIM:(h + 1) * HEAD_DIM]
            rot = pltpu.roll(t, HEAD_DIM // 2, axis=1)
            o_ref[:, h * HEAD_DIM:(h + 1) * HEAD_DIM] = (t * cos + rot * sin).astype(BF16)

    @pl.when(j == 0)
    def _():
        rope(MOBA_Q_SCALE)

    @pl.when(is_rope & (j != 0))
    def _():
        rope(None)

    @pl.when(jnp.logical_not(is_rope))
    def _():
        o_ref[...] = acc.astype(BF16)


def _inproj(x2, norm_w, w_in_bf16, cos_full, sin_signed, seq, tm):
    tokens, d_model = x2.shape
    in_cols = w_in_bf16.shape[1]
    n_groups = in_cols // GROUP_WIDTH
    tiles_per_seq = seq // tm
    return pl.pallas_call(
        _inproj_kernel,
        grid=(tokens // tm, n_groups),
        in_specs=[
            pl.BlockSpec((tm, d_model), lambda i, j: (i, 0)),
            pl.BlockSpec((1, d_model), lambda i, j: (0, 0)),
            pl.BlockSpec((d_model, GROUP_WIDTH), lambda i, j: (0, j)),
            pl.BlockSpec((tm, HEAD_DIM), lambda i, j: (i % tiles_per_seq, 0)),
            pl.BlockSpec((tm, HEAD_DIM), lambda i, j: (i % tiles_per_seq, 0)),
        ],
        out_specs=pl.BlockSpec((tm, GROUP_WIDTH), lambda i, j: (i, j)),
        out_shape=jax.ShapeDtypeStruct((tokens, in_cols), BF16),
        scratch_shapes=[pltpu.VMEM((tm, d_model), BF16)],
        compiler_params=pltpu.CompilerParams(
            dimension_semantics=("arbitrary", "arbitrary"),
            vmem_limit_bytes=V7X_VMEM_LIMIT_BYTES),
        name="inproj",
    )(x2, norm_w, w_in_bf16, cos_full, sin_signed)


def _moba_kernel(q_ref, k_ref, v_ref, o_ref, kaug_ref, vt_ref, kmean_ref, qt_ref, s_ref, p_ref):
    seq = q_ref.shape[0]
    nb = seq // MOBA_BLOCK

    lane = lax.broadcasted_iota(jnp.int32, (MOBA_BLOCK, HEAD_DIM), 1)
    for jb in range(nb):
        rows = slice(jb * MOBA_BLOCK, (jb + 1) * MOBA_BLOCK)
        kb = k_ref[rows, :]
        kaug_ref[rows, :HEAD_DIM] = kb
        kaug_ref[rows, HEAD_DIM:] = jnp.where(lane == jb, 1.0, 0.0).astype(BF16)
        vt_ref[:, rows] = v_ref[rows, :].astype(F32).T.astype(BF16)
        kmean_ref[jb:jb + 1, :] = jnp.mean(kb.astype(F32), axis=0, keepdims=True)
    qt_ref[HEAD_DIM:, :] = jnp.zeros((HEAD_DIM, MOBA_BLOCK), BF16)
    kmean = kmean_ref[...].astype(BF16)

    blk_idx = lax.broadcasted_iota(jnp.int32, (nb, MOBA_BLOCK), 0)
    key_pos = lax.broadcasted_iota(jnp.int32, (MOBA_BLOCK, MOBA_BLOCK), 0)
    qry_pos = lax.broadcasted_iota(jnp.int32, (MOBA_BLOCK, MOBA_BLOCK), 1)
    causal = key_pos <= qry_pos

    for i in range(nb):
        q_rows = slice(i * MOBA_BLOCK, (i + 1) * MOBA_BLOCK)
        qt = q_ref[q_rows, :].astype(F32).T.astype(BF16)

        past = blk_idx < i
        gate = jnp.where(past, _dot(kmean, qt), -jnp.inf)
        beaten_by = jnp.zeros((nb, MOBA_BLOCK), jnp.int32)
        for jp in range(i):
            row = gate[jp:jp + 1, :]
            beats = (row > gate) | ((row == gate) & (jp < blk_idx))
            beaten_by = beaten_by + beats.astype(jnp.int32)
        visible = (past & (beaten_by < MOBA_TOPK)) | (blk_idx == i)
        qt_ref[:HEAD_DIM, :] = qt
        qt_ref[HEAD_DIM:HEAD_DIM + nb, :] = jnp.where(visible, 0.0, MASKED_SCORE).astype(BF16)
        qt_aug = qt_ref[...]

        m = None
        for t in range(i + 1):
            k_rows = slice(t * MOBA_BLOCK, (t + 1) * MOBA_BLOCK)
            s = _dot(kaug_ref[k_rows, :], qt_aug)
            if t == i:
                s = jnp.where(causal, s, MASKED_SCORE)
            s_ref[k_rows, :] = s
            col_max = jnp.max(s, axis=0, keepdims=True)
            m = col_max if m is None else jnp.maximum(m, col_max)
        l = None
        for t in range(i + 1):
            k_rows = slice(t * MOBA_BLOCK, (t + 1) * MOBA_BLOCK)
            p = jnp.exp2(s_ref[k_rows, :] - m)
            col_sum = jnp.sum(p, axis=0, keepdims=True)
            l = col_sum if l is None else l + col_sum
            p_ref[k_rows, :] = p.astype(BF16)
        n_keys = (i + 1) * MOBA_BLOCK
        out_t = _dot(vt_ref[:, :n_keys], p_ref[:n_keys, :]) / l
        o_ref[q_rows, :] = out_t.T.astype(BF16)


def _moba(proj3):
    batch, seq, _ = proj3.shape
    nb = seq // MOBA_BLOCK
    assert nb <= HEAD_DIM, "block-selection rows must fit in the spare contraction rows"

    def col(group):
        return pl.BlockSpec((None, seq, HEAD_DIM), lambda b, h: (b, 0, group * N_HEADS + h))

    return pl.pallas_call(
        _moba_kernel,
        grid=(batch, N_HEADS),
        in_specs=[col(0), col(1), col(2)],
        out_specs=pl.BlockSpec((None, seq, HEAD_DIM), lambda b, h: (b, 0, h)),
        out_shape=jax.ShapeDtypeStruct((batch, seq, GROUP_WIDTH), BF16),
        scratch_shapes=[
            pltpu.VMEM((seq, 2 * HEAD_DIM), BF16),
            pltpu.VMEM((HEAD_DIM, seq), BF16),
            pltpu.VMEM((nb, HEAD_DIM), F32),
            pltpu.VMEM((2 * HEAD_DIM, MOBA_BLOCK), BF16),
            pltpu.VMEM((seq, MOBA_BLOCK), F32),
            pltpu.VMEM((seq, MOBA_BLOCK), BF16),
        ],
        compiler_params=pltpu.CompilerParams(
            dimension_semantics=("arbitrary", "arbitrary"),
            vmem_limit_bytes=V7X_VMEM_LIMIT_BYTES),
        name="moba",
    )(proj3, proj3, proj3)


def _retention_kernel(logg_ref, q_ref, k_ref, v_ref, g_ref, gw_ref, gb_ref, o_ref, state_ref):
    seq = q_ref.shape[0]
    nc = seq // RET_CHUNK
    log_g = logg_ref[pl.program_id(1)]

    row = lax.broadcasted_iota(jnp.int32, (RET_CHUNK, RET_CHUNK), 0)
    colm = lax.broadcasted_iota(jnp.int32, (RET_CHUNK, RET_CHUNK), 1)
    rel = (row - colm).astype(F32)
    decay = jnp.where(rel >= 0, jnp.exp(log_g * jnp.maximum(rel, 0.0)), 0.0) * ATTN_SCALE
    pos = lax.broadcasted_iota(jnp.int32, (RET_CHUNK, HEAD_DIM), 0).astype(F32)
    xi = jnp.exp(log_g * (pos + 1.0))
    zeta = jnp.exp(log_g * (RET_CHUNK - 1.0 - pos)) * ATTN_SCALE
    g_chunk = jnp.exp(log_g * RET_CHUNK)
    gn_w = gw_ref[...]
    gn_b = gb_ref[...]

    state_ref[...] = jnp.zeros_like(state_ref)

    def chunk(c, _):
        r0 = pl.multiple_of(c * RET_CHUNK, RET_CHUNK)
        rows = pl.ds(r0, RET_CHUNK)
        qc = q_ref[rows, :]
        kc = k_ref[rows, :]
        vc = v_ref[rows, :]
        state = state_ref[...]
        inner = _dot_nt(qc, kc) * decay
        y = _dot(inner.astype(BF16), vc) + _dot(qc, state.astype(BF16)) * xi
        kz_t = (kc.astype(F32) * zeta).T.astype(BF16)
        state_ref[...] = state * g_chunk + _dot(kz_t, vc)
        mu = jnp.mean(y, axis=-1, keepdims=True)
        yc = y - mu
        var = jnp.mean(yc * yc, axis=-1, keepdims=True)
        r = yc * lax.rsqrt(var + GN_EPS) * gn_w + gn_b
        g = g_ref[rows, :].astype(F32)
        o_ref[rows, :] = (g * jax.nn.sigmoid(g) * r).astype(BF16)
        return 0

    lax.fori_loop(0, nc, chunk, 0)


def _retention(proj3, log_g, gn_w, gn_b):
    batch, seq, _ = proj3.shape

    def col(group):
        return pl.BlockSpec((None, seq, HEAD_DIM), lambda b, h, lg: (b, 0, group * N_HEADS + h))

    head_vec = pl.BlockSpec((1, HEAD_DIM), lambda b, h, lg: (0, h))
    return pl.pallas_call(
        _retention_kernel,
        grid_spec=pltpu.PrefetchScalarGridSpec(
            num_scalar_prefetch=1,
            grid=(batch, N_HEADS),
            in_specs=[col(3), col(4), col(5), col(6), head_vec, head_vec],
            out_specs=pl.BlockSpec((None, seq, HEAD_DIM), lambda b, h, lg: (b, 0, h)),
            scratch_shapes=[pltpu.VMEM((HEAD_DIM, HEAD_DIM), F32)],
        ),
        out_shape=jax.ShapeDtypeStruct((batch, seq, GROUP_WIDTH), BF16),
        compiler_params=pltpu.CompilerParams(
            dimension_semantics=("arbitrary", "arbitrary"),
            vmem_limit_bytes=V7X_VMEM_LIMIT_BYTES),
        name="retention",
    )(log_g, proj3, proj3, proj3, proj3, gn_w, gn_b)


def _outproj_kernel(a_ref, r_ref, w_ref, x_ref, post_ref, pre_ref, x1_ref, hn_ref):
    mix = _dot(a_ref[...], w_ref[:GROUP_WIDTH, :]) + _dot(r_ref[...], w_ref[GROUP_WIDTH:, :])
    x1 = x_ref[...] + mix * _rms_scale(mix) * post_ref[...]
    x1_ref[...] = x1
    hn_ref[...] = (x1 * _rms_scale(x1) * pre_ref[...]).astype(BF16)


def _outproj(a2, r2, w_out_bf16, x2, norm_post, norm_pre, tm):
    tokens, d_model = x2.shape
    row_tile = lambda width: pl.BlockSpec((tm, width), lambda i: (i, 0))
    whole = lambda shape: pl.BlockSpec(shape, lambda i: (0, 0))
    return pl.pallas_call(
        _outproj_kernel,
        grid=(tokens // tm,),
        in_specs=[row_tile(GROUP_WIDTH), row_tile(GROUP_WIDTH), whole(w_out_bf16.shape),
                  row_tile(d_model), whole((1, d_model)), whole((1, d_model))],
        out_specs=[row_tile(d_model), row_tile(d_model)],
        out_shape=[jax.ShapeDtypeStruct((tokens, d_model), F32),
                   jax.ShapeDtypeStruct((tokens, d_model), BF16)],
        compiler_params=pltpu.CompilerParams(
            dimension_semantics=("arbitrary",),
            vmem_limit_bytes=V7X_VMEM_LIMIT_BYTES),
        name="outproj",
    )(a2, r2, w_out_bf16, x2, norm_post, norm_pre)


def _ffn_kernel(hn_ref, wg_ref, wv_ref, cw_ref, cb_ref, wd_ref, x1_ref, post_ref,
                o_ref, acc_ref, carry_ref, *, tiles_per_seq):
    i = pl.program_id(0)
    j = pl.program_id(1)
    tm = hn_ref.shape[0]
    hn = hn_ref[...]
    ug = _dot(hn, wg_ref[...])
    uv = _dot(hn, wv_ref[...])

    @pl.when(i % tiles_per_seq == 0)
    def _():
        carry_ref[j] = jnp.zeros(carry_ref.shape[1:], F32)

    prev = carry_ref[j]
    carry_ref[j] = ug[tm - 8:, :]
    row = lax.broadcasted_iota(jnp.int32, ug.shape, 0)
    back1 = jnp.where(row == 0, prev[7:8, :], pltpu.roll(ug, 1, axis=0))
    back2 = jnp.where(row == 0, prev[6:7, :],
                      jnp.where(row == 1, prev[7:8, :], pltpu.roll(ug, 2, axis=0)))
    cw = cw_ref[...]
    conv = cw[0:1, :] * back2 + cw[1:2, :] * back1 + cw[2:3, :] * ug + cb_ref[...]
    hidden = (conv * jax.nn.sigmoid(conv) * uv).astype(BF16)
    part = _dot(hidden, wd_ref[...])

    @pl.when(j == 0)
    def _():
        acc_ref[...] = part

    @pl.when(j > 0)
    def _():
        acc_ref[...] += part

    @pl.when(j == pl.num_programs(1) - 1)
    def _():
        ff = acc_ref[...]
        o_ref[...] = x1_ref[...] + ff * _rms_scale(ff) * post_ref[...]


def _ffn(hn2, w_up_bf16, conv_w, conv_b, w_down_bf16, x1, norm_post, seq, tm, tf):
    tokens, d_model = x1.shape
    d_ff = w_down_bf16.shape[0]
    n_chunks = d_ff // tf
    row_tile = pl.BlockSpec((tm, d_model), lambda i, j: (i, 0))
    return pl.pallas_call(
        functools.partial(_ffn_kernel, tiles_per_seq=seq // tm),
        grid=(tokens // tm, n_chunks),
        in_specs=[
            row_tile,
            pl.BlockSpec((d_model, tf), lambda i, j: (0, j)),
            pl.BlockSpec((d_model, tf), lambda i, j: (0, n_chunks + j)),
            pl.BlockSpec((CONV_WIDTH, tf), lambda i, j: (0, j)),
            pl.BlockSpec((1, tf), lambda i, j: (0, j)),
            pl.BlockSpec((tf, d_model), lambda i, j: (j, 0)),
            row_tile,
            pl.BlockSpec((1, d_model), lambda i, j: (0, 0)),
        ],
        out_specs=row_tile,
        out_shape=jax.ShapeDtypeStruct((tokens, d_model), F32),
        scratch_shapes=[pltpu.VMEM((tm, d_model), F32),
                        pltpu.VMEM((n_chunks, 8, tf), F32)],
        compiler_params=pltpu.CompilerParams(
            dimension_semantics=("arbitrary", "arbitrary"),
            vmem_limit_bytes=V7X_VMEM_LIMIT_BYTES),
        name="ffn",
    )(hn2, w_up_bf16, w_up_bf16, conv_w, conv_b, w_down_bf16, x1, norm_post)


def _rope_tables(seq):
    pos = jnp.arange(seq, dtype=F32)
    inv = 1.0 / (ROPE_THETA ** (jnp.arange(0, HEAD_DIM, 2, dtype=F32) / HEAD_DIM))
    ang = pos[:, None] * inv[None, :]
    cos, sin = jnp.cos(ang), jnp.sin(ang)
    return jnp.concatenate([cos, cos], axis=-1), jnp.concatenate([-sin, sin], axis=-1)


def _tile(n, preferred):
    for t in preferred:
        if n % t == 0:
            return t
    raise ValueError(f"no supported tile for extent {n}")


def kernel(x, norm_mix_pre, w_in, ret_gn_w, ret_gn_b, w_out, norm_mix_post, norm_ffn_pre, w_up, conv_w, conv_b, w_down, norm_ffn_post):
    batch, seq, d_model = x.shape
    depth = w_in.shape[0]
    assert seq % MOBA_BLOCK == 0 and seq % RET_CHUNK == 0
    assert w_in.shape[2] == 7 * GROUP_WIDTH and w_out.shape[1] == 2 * GROUP_WIDTH
    d_ff = w_down.shape[1]
    tokens = batch * seq
    tm_proj = _tile(seq, (1024, 512, 256))
    tm_row = _tile(seq, (512, 256))
    tf = _tile(d_ff, (512, 256, 128))

    cos_full, sin_signed = _rope_tables(seq)
    log_g = jnp.log(1.0 - 2.0 ** (-5.0 - jnp.arange(N_HEADS, dtype=F32)))

    x2 = x.reshape(tokens, d_model)
    for layer in range(depth):
        proj = _inproj(x2, norm_mix_pre[layer][None, :], w_in[layer].astype(BF16),
                       cos_full, sin_signed, seq, tm_proj)
        proj3 = proj.reshape(batch, seq, -1)
        a = _moba(proj3)
        r = _retention(proj3, log_g, ret_gn_w[layer][None, :], ret_gn_b[layer][None, :])
        x1, hn2 = _outproj(a.reshape(tokens, GROUP_WIDTH), r.reshape(tokens, GROUP_WIDTH),
                           w_out[layer].astype(BF16), x2,
                           norm_mix_post[layer][None, :], norm_ffn_pre[layer][None, :], tm_row)
        x2 = _ffn(hn2, w_up[layer].astype(BF16), conv_w[layer], conv_b[layer][None, :],
                  w_down[layer].astype(BF16), x1, norm_ffn_post[layer][None, :], seq, tm_row, tf)
    return x2.reshape(batch, seq, d_model)
```

```python
import functools

import jax
import jax.numpy as jnp
from jax import lax
from jax.experimental import pallas as pl
from jax.experimental.pallas import tpu as pltpu

HEAD_DIM = 128
N_HEADS = 8
GROUP_WIDTH = N_HEADS * HEAD_DIM
MOBA_BLOCK = 256
MOBA_TOPK = 3
RET_CHUNK = 256
ROPE_THETA = 10000.0
CONV_WIDTH = 3
NORM_EPS = 1e-6
GN_EPS = 1e-5
ATTN_SCALE = HEAD_DIM ** -0.5
MASKED_SCORE = -1e30
MOBA_Q_SCALE = ATTN_SCALE * 1.4426950408889634

V7X_VMEM_LIMIT_BYTES = 56 * 1024 * 1024
SUB_ROWS = 256

F32 = jnp.float32
BF16 = jnp.bfloat16


def _dot(a, b):
    return jnp.dot(a, b, preferred_element_type=F32)


def _dot_nt(a, b):
    return lax.dot_general(a, b, (((1,), (1,)), ((), ())), preferred_element_type=F32)


def _rms_scale(v):
    return lax.rsqrt(jnp.mean(v * v, axis=-1, keepdims=True) + NORM_EPS)


def _sub_tiles(rows):
    return [slice(r, r + SUB_ROWS) for r in range(0, rows, SUB_ROWS)]


def _inproj_kernel(x_ref, nw_ref, w_ref, a_ref, b_ref, o_ref, hn_ref):
    j = pl.program_id(1)

    def body(first):
        for rows in _sub_tiles(x_ref.shape[0]):
            if first:
                x = x_ref[rows, :]
                hn = (x * _rms_scale(x) * nw_ref[...]).astype(BF16)
                hn_ref[rows, :] = hn
            else:
                hn = hn_ref[rows, :]
            acc = _dot(hn, w_ref[...])
            a = a_ref[rows, :]
            b = b_ref[rows, :]
            for h in range(N_HEADS):
                cols = slice(h * HEAD_DIM, (h + 1) * HEAD_DIM)
                t = acc[:, cols]
                rot = pltpu.roll(t, HEAD_DIM // 2, axis=1)
                o_ref[rows, cols] = (t * a + rot * b).astype(BF16)

    pl.when(j == 0)(lambda: body(True))
    pl.when(j > 0)(lambda: body(False))


def _inproj(x2, norm_w, w_in_bf16, tab_a, tab_b, seq, tm):
    tokens, d_model = x2.shape
    in_cols = w_in_bf16.shape[1]
    n_groups = in_cols // GROUP_WIDTH
    tiles_per_seq = seq // tm

    def table(i, j):
        rotary = (j == 1) | (j == 3) | (j == 4)
        kind = jnp.where(j == 0, 0, jnp.where(rotary, 1, 2))
        return (kind, i % tiles_per_seq, 0)

    return pl.pallas_call(
        _inproj_kernel,
        grid=(tokens // tm, n_groups),
        in_specs=[
            pl.BlockSpec((tm, d_model), lambda i, j: (i, 0)),
            pl.BlockSpec((1, d_model), lambda i, j: (0, 0)),
            pl.BlockSpec((d_model, GROUP_WIDTH), lambda i, j: (0, j)),
            pl.BlockSpec((None, tm, HEAD_DIM), table),
            pl.BlockSpec((None, tm, HEAD_DIM), table),
        ],
        out_specs=pl.BlockSpec((tm, GROUP_WIDTH), lambda i, j: (i, j)),
        out_shape=jax.ShapeDtypeStruct((tokens, in_cols), BF16),
        scratch_shapes=[pltpu.VMEM((tm, d_model), BF16)],
        compiler_params=pltpu.CompilerParams(
            dimension_semantics=("arbitrary", "arbitrary"),
            vmem_limit_bytes=V7X_VMEM_LIMIT_BYTES),
        name="inproj",
    )(x2, norm_w, w_in_bf16, tab_a, tab_b)


def _moba_kernel(q_ref, k_ref, v_ref, o_ref, kaug_ref, vt_ref, kmean_ref, qt_ref, s_ref, p_ref):
    seq = q_ref.shape[0]
    nb = seq // MOBA_BLOCK

    lane = lax.broadcasted_iota(jnp.int32, (MOBA_BLOCK, HEAD_DIM), 1)
    for jb in range(nb):
        rows = slice(jb * MOBA_BLOCK, (jb + 1) * MOBA_BLOCK)
        kb = k_ref[rows, :]
        kaug_ref[rows, :HEAD_DIM] = kb
        kaug_ref[rows, HEAD_DIM:] = jnp.where(lane == jb, 1.0, 0.0).astype(BF16)
        vt_ref[:, rows] = v_ref[rows, :].astype(F32).T.astype(BF16)
        kmean_ref[jb:jb + 1, :] = jnp.mean(kb.astype(F32), axis=0, keepdims=True)
    for slot in range(2):
        qt_ref[slot, HEAD_DIM:, :] = jnp.zeros((HEAD_DIM, MOBA_BLOCK), BF16)
    kmean = kmean_ref[...].astype(BF16)

    blk_idx = lax.broadcasted_iota(jnp.int32, (nb, MOBA_BLOCK), 0)
    key_pos = lax.broadcasted_iota(jnp.int32, (MOBA_BLOCK, MOBA_BLOCK), 0)
    qry_pos = lax.broadcasted_iota(jnp.int32, (MOBA_BLOCK, MOBA_BLOCK), 1)
    causal = key_pos <= qry_pos

    col_max = {}

    def score_stage(i):
        slot = i % 2
        qt = q_ref[i * MOBA_BLOCK:(i + 1) * MOBA_BLOCK, :].astype(F32).T.astype(BF16)
        past = blk_idx < i
        gate = jnp.where(past, _dot(kmean, qt), -jnp.inf)
        beaten_by = jnp.zeros((nb, MOBA_BLOCK), jnp.int32)
        for jp in range(i):
            row = gate[jp:jp + 1, :]
            beats = (row > gate) | ((row == gate) & (jp < blk_idx))
            beaten_by = beaten_by + beats.astype(jnp.int32)
        visible = (past & (beaten_by < MOBA_TOPK)) | (blk_idx == i)
        qt_ref[slot, :HEAD_DIM, :] = qt
        qt_ref[slot, HEAD_DIM:HEAD_DIM + nb, :] = jnp.where(visible, 0.0, MASKED_SCORE).astype(BF16)
        qt_aug = qt_ref[slot]
        yield
        m = None
        for t in range(i + 1):
            k_rows = slice(t * MOBA_BLOCK, (t + 1) * MOBA_BLOCK)
            s = _dot(kaug_ref[k_rows, :], qt_aug)
            if t == i:
                s = jnp.where(causal, s, MASKED_SCORE)
            s_ref[slot, k_rows, :] = s
            tile_max = jnp.max(s, axis=0, keepdims=True)
            m = tile_max if m is None else jnp.maximum(m, tile_max)
            yield
        col_max[i] = m

    def value_stage(i):
        slot = i % 2
        m = col_max.pop(i)
        l = None
        for t in range(i + 1):
            k_rows = slice(t * MOBA_BLOCK, (t + 1) * MOBA_BLOCK)
            p = jnp.exp2(s_ref[slot, k_rows, :] - m)
            tile_sum = jnp.sum(p, axis=0, keepdims=True)
            l = tile_sum if l is None else l + tile_sum
            p_ref[slot, k_rows, :] = p.astype(BF16)
            yield
        n_keys = (i + 1) * MOBA_BLOCK
        out_t = _dot(vt_ref[:, :n_keys], p_ref[slot, :n_keys, :]) / l
        o_ref[i * MOBA_BLOCK:(i + 1) * MOBA_BLOCK, :] = out_t.T.astype(BF16)
        yield

    for _ in score_stage(0):
        pass
    for i in range(nb):
        stages = [value_stage(i)] + ([score_stage(i + 1)] if i + 1 < nb else [])
        while stages:
            for stage in list(stages):
                if next(stage, "done") == "done":
                    stages.remove(stage)


def _moba(proj3):
    batch, seq, _ = proj3.shape
    nb = seq // MOBA_BLOCK
    assert nb <= HEAD_DIM, "block-selection rows must fit in the spare contraction rows"

    def col(group):
        return pl.BlockSpec((None, seq, HEAD_DIM), lambda b, h: (b, 0, group * N_HEADS + h))

    return pl.pallas_call(
        _moba_kernel,
        grid=(batch, N_HEADS),
        in_specs=[col(0), col(1), col(2)],
        out_specs=pl.BlockSpec((None, seq, HEAD_DIM), lambda b, h: (b, 0, h)),
        out_shape=jax.ShapeDtypeStruct((batch, seq, GROUP_WIDTH), BF16),
        scratch_shapes=[
            pltpu.VMEM((seq, 2 * HEAD_DIM), BF16),
            pltpu.VMEM((HEAD_DIM, seq), BF16),
            pltpu.VMEM((nb, HEAD_DIM), F32),
            pltpu.VMEM((2, 2 * HEAD_DIM, MOBA_BLOCK), BF16),
            pltpu.VMEM((2, seq, MOBA_BLOCK), F32),
            pltpu.VMEM((2, seq, MOBA_BLOCK), BF16),
        ],
        compiler_params=pltpu.CompilerParams(
            dimension_semantics=("arbitrary", "arbitrary"),
            vmem_limit_bytes=V7X_VMEM_LIMIT_BYTES),
        name="moba",
    )(proj3, proj3, proj3)


def _retention_kernel(logg_ref, q_ref, k_ref, v_ref, g_ref, gw_ref, gb_ref, o_ref):
    seq = q_ref.shape[0]
    log_g = logg_ref[pl.program_id(1)]

    row = lax.broadcasted_iota(jnp.int32, (RET_CHUNK, RET_CHUNK), 0)
    colm = lax.broadcasted_iota(jnp.int32, (RET_CHUNK, RET_CHUNK), 1)
    rel = (row - colm).astype(F32)
    decay = jnp.where(rel >= 0, jnp.exp(log_g * jnp.maximum(rel, 0.0)), 0.0) * ATTN_SCALE
    pos = lax.broadcasted_iota(jnp.int32, (RET_CHUNK, HEAD_DIM), 0).astype(F32)
    xi = jnp.exp(log_g * (pos + 1.0))
    zeta = jnp.exp(log_g * (RET_CHUNK - 1.0 - pos)) * ATTN_SCALE
    g_chunk = jnp.exp(log_g * RET_CHUNK)
    gn_w = gw_ref[...]
    gn_b = gb_ref[...]

    state = jnp.zeros((HEAD_DIM, HEAD_DIM), F32)
    for c in range(seq // RET_CHUNK):
        rows = slice(c * RET_CHUNK, (c + 1) * RET_CHUNK)
        qc = q_ref[rows, :]
        kc = k_ref[rows, :]
        vc = v_ref[rows, :]
        inner = _dot_nt(qc, kc) * decay
        y = _dot(inner.astype(BF16), vc)
        if c > 0:
            y = y + _dot(qc, state.astype(BF16)) * xi
        kz_t = (kc.astype(F32) * zeta).T.astype(BF16)
        state = state * g_chunk + _dot(kz_t, vc)
        mu = jnp.mean(y, axis=-1, keepdims=True)
        yc = y - mu
        var = jnp.mean(yc * yc, axis=-1, keepdims=True)
        r = yc * lax.rsqrt(var + GN_EPS) * gn_w + gn_b
        g = g_ref[rows, :].astype(F32)
        o_ref[rows, :] = (g * jax.nn.sigmoid(g) * r).astype(BF16)


def _retention(proj3, log_g, gn_w, gn_b):
    batch, seq, _ = proj3.shape

    def col(group):
        return pl.BlockSpec((None, seq, HEAD_DIM), lambda b, h, lg: (b, 0, group * N_HEADS + h))

    head_vec = pl.BlockSpec((1, HEAD_DIM), lambda b, h, lg: (0, h))
    return pl.pallas_call(
        _retention_kernel,
        grid_spec=pltpu.PrefetchScalarGridSpec(
            num_scalar_prefetch=1,
            grid=(batch, N_HEADS),
            in_specs=[col(3), col(4), col(5), col(6), head_vec, head_vec],
            out_specs=pl.BlockSpec((None, seq, HEAD_DIM), lambda b, h, lg: (b, 0, h)),
        ),
        out_shape=jax.ShapeDtypeStruct((batch, seq, GROUP_WIDTH), BF16),
        compiler_params=pltpu.CompilerParams(
            dimension_semantics=("arbitrary", "arbitrary"),
            vmem_limit_bytes=V7X_VMEM_LIMIT_BYTES),
        name="retention",
    )(log_g, proj3, proj3, proj3, proj3, gn_w, gn_b)


def _outproj_kernel(a_ref, r_ref, w_ref, x_ref, post_ref, pre_ref, x1_ref, hn_ref):
    for rows in _sub_tiles(x_ref.shape[0]):
        mix = (_dot(a_ref[rows, :], w_ref[:GROUP_WIDTH, :])
               + _dot(r_ref[rows, :], w_ref[GROUP_WIDTH:, :]))
        x1 = x_ref[rows, :] + mix * _rms_scale(mix) * post_ref[...]
        x1_ref[rows, :] = x1
        hn_ref[rows, :] = (x1 * _rms_scale(x1) * pre_ref[...]).astype(BF16)


def _outproj(a2, r2, w_out_bf16, x2, norm_post, norm_pre, tm):
    tokens, d_model = x2.shape
    row_tile = lambda width: pl.BlockSpec((tm, width), lambda i: (i, 0))
    whole = lambda shape: pl.BlockSpec(shape, lambda i: (0, 0))
    return pl.pallas_call(
        _outproj_kernel,
        grid=(tokens // tm,),
        in_specs=[row_tile(GROUP_WIDTH), row_tile(GROUP_WIDTH), whole(w_out_bf16.shape),
                  row_tile(d_model), whole((1, d_model)), whole((1, d_model))],
        out_specs=[row_tile(d_model), row_tile(d_model)],
        out_shape=[jax.ShapeDtypeStruct((tokens, d_model), F32),
                   jax.ShapeDtypeStruct((tokens, d_model), BF16)],
        compiler_params=pltpu.CompilerParams(
            dimension_semantics=("arbitrary",),
            vmem_limit_bytes=V7X_VMEM_LIMIT_BYTES),
        name="outproj",
    )(a2, r2, w_out_bf16, x2, norm_post, norm_pre)


def _ffn_kernel(hn_ref, wg_ref, wv_ref, cw_ref, cb_ref, wd_ref, x1_ref, post_ref,
                o_ref, acc_ref, carry_ref, *, tiles_per_seq):
    i = pl.program_id(0)
    j = pl.program_id(1)
    last = pl.num_programs(1) - 1
    tf = wg_ref.shape[1]

    @pl.when(i % tiles_per_seq == 0)
    def _():
        carry_ref[j] = jnp.zeros((8, tf), F32)

    def body(first_chunk, last_chunk):
        cw = cw_ref[...]
        cb = cb_ref[...]
        history = carry_ref[j]
        prev2, prev1 = history[6:7, :], history[7:8, :]
        row = lax.broadcasted_iota(jnp.int32, (SUB_ROWS, tf), 0)
        for rows in _sub_tiles(hn_ref.shape[0]):
            hn = hn_ref[rows, :]
            ug = _dot(hn, wg_ref[...])
            uv = _dot(hn, wv_ref[...])
            back1 = jnp.where(row == 0, prev1, pltpu.roll(ug, 1, axis=0))
            back2 = jnp.where(row == 0, prev2,
                              jnp.where(row == 1, prev1, pltpu.roll(ug, 2, axis=0)))
            conv = cw[0:1, :] * back2 + cw[1:2, :] * back1 + cw[2:3, :] * ug + cb
            hidden = (conv * jax.nn.sigmoid(conv) * uv).astype(BF16)
            part = _dot(hidden, wd_ref[...])
            ff = part if first_chunk else acc_ref[rows, :] + part
            if last_chunk:
                o_ref[rows, :] = x1_ref[rows, :] + ff * _rms_scale(ff) * post_ref[...]
            else:
                acc_ref[rows, :] = ff
            prev2, prev1 = ug[SUB_ROWS - 2:SUB_ROWS - 1, :], ug[SUB_ROWS - 1:, :]
            tail = ug[SUB_ROWS - 8:, :]
        carry_ref[j] = tail

    pl.when(j == 0)(lambda: body(True, False))
    pl.when((j > 0) & (j < last))(lambda: body(False, False))
    pl.when(j == last)(lambda: body(False, True))


def _ffn(hn2, w_up_bf16, conv_w, conv_b, w_down_bf16, x1, norm_post, seq, tm, tf):
    tokens, d_model = x1.shape
    d_ff = w_down_bf16.shape[0]
    n_chunks = d_ff // tf
    assert n_chunks >= 2
    row_tile = pl.BlockSpec((tm, d_model), lambda i, j: (i, 0))
    return pl.pallas_call(
        functools.partial(_ffn_kernel, tiles_per_seq=seq // tm),
        grid=(tokens // tm, n_chunks),
        in_specs=[
            row_tile,
            pl.BlockSpec((d_model, tf), lambda i, j: (0, j)),
            pl.BlockSpec((d_model, tf), lambda i, j: (0, n_chunks + j)),
            pl.BlockSpec((CONV_WIDTH, tf), lambda i, j: (0, j)),
            pl.BlockSpec((1, tf), lambda i, j: (0, j)),
            pl.BlockSpec((tf, d_model), lambda i, j: (j, 0)),
            row_tile,
            pl.BlockSpec((1, d_model), lambda i, j: (0, 0)),
        ],
        out_specs=row_tile,
        out_shape=jax.ShapeDtypeStruct((tokens, d_model), F32),
        scratch_shapes=[pltpu.VMEM((tm, d_model), F32),
                        pltpu.VMEM((n_chunks, 8, tf), F32)],
        compiler_params=pltpu.CompilerParams(
            dimension_semantics=("arbitrary", "arbitrary"),
            vmem_limit_bytes=V7X_VMEM_LIMIT_BYTES),
        name="ffn",
    )(hn2, w_up_bf16, w_up_bf16, conv_w, conv_b, w_down_bf16, x1, norm_post)


def _rope_tables(seq):
    pos = jnp.arange(seq, dtype=F32)
    inv = 1.0 / (ROPE_THETA ** (jnp.arange(0, HEAD_DIM, 2, dtype=F32) / HEAD_DIM))
    ang = pos[:, None] * inv[None, :]
    cos, sin = jnp.cos(ang), jnp.sin(ang)
    cos_full = jnp.concatenate([cos, cos], axis=-1)
    sin_signed = jnp.concatenate([-sin, sin], axis=-1)
    tab_a = jnp.stack([cos_full * MOBA_Q_SCALE, cos_full, jnp.ones_like(cos_full)])
    tab_b = jnp.stack([sin_signed * MOBA_Q_SCALE, sin_signed, jnp.zeros_like(sin_signed)])
    return tab_a, tab_b


def _tile(n, preferred):
    for t in preferred:
        if n % t == 0:
            return t
    raise ValueError(f"no supported tile for extent {n}")


def kernel(x, norm_mix_pre, w_in, ret_gn_w, ret_gn_b, w_out, norm_mix_post, norm_ffn_pre, w_up, conv_w, conv_b, w_down, norm_ffn_post):
    batch, seq, d_model = x.shape
    depth = w_in.shape[0]
    assert seq % MOBA_BLOCK == 0 and seq % RET_CHUNK == 0
    assert w_in.shape[2] == 7 * GROUP_WIDTH and w_out.shape[1] == 2 * GROUP_WIDTH
    d_ff = w_down.shape[1]
    tokens = batch * seq
    tm_proj = _tile(seq, (1024, 512, 256))
    tm_row = _tile(seq, (512, 256))
    tf = _tile(d_ff, (512, 256, 128))

    tab_a, tab_b = _rope_tables(seq)
    log_g = jnp.log(1.0 - 2.0 ** (-5.0 - jnp.arange(N_HEADS, dtype=F32)))

    x2 = x.reshape(tokens, d_model)
    for layer in range(depth):
        proj = _inproj(x2, norm_mix_pre[layer][None, :], w_in[layer].astype(BF16),
                       tab_a, tab_b, seq, tm_proj)
        proj3 = proj.reshape(batch, seq, -1)
        a = _moba(proj3)
        r = _retention(proj3, log_g, ret_gn_w[layer][None, :], ret_gn_b[layer][None, :])
        x1, hn2 = _outproj(a.reshape(tokens, GROUP_WIDTH), r.reshape(tokens, GROUP_WIDTH),
                           w_out[layer].astype(BF16), x2,
                           norm_mix_post[layer][None, :], norm_ffn_pre[layer][None, :], tm_row)
        x2 = _ffn(hn2, w_up[layer].astype(BF16), conv_w[layer], conv_b[layer][None, :],
                  w_down[layer].astype(BF16), x1, norm_ffn_post[layer][None, :], seq, tm_row, tf)
    return x2.reshape(batch, seq, d_model)
```

```python
import functools

import jax
import jax.numpy as jnp
from jax import lax
from jax.experimental import pallas as pl
from jax.experimental.pallas import tpu as pltpu

HEAD_DIM = 128
N_HEADS = 8
GROUP_WIDTH = N_HEADS * HEAD_DIM
MOBA_BLOCK = 256
MOBA_TOPK = 3
RET_CHUNK = 256
ROPE_THETA = 10000.0
CONV_WIDTH = 3
NORM_EPS = 1e-6
GN_EPS = 1e-5
ATTN_SCALE = HEAD_DIM ** -0.5
MASKED_SCORE = -1e30
MOBA_Q_SCALE = ATTN_SCALE * 1.4426950408889634

V7X_VMEM_LIMIT_BYTES = 56 * 1024 * 1024
SUB_ROWS = 256
X1_SLAB_COLS = 256
BF16_SUBLANES = 16

F32 = jnp.float32
BF16 = jnp.bfloat16


def _dot(a, b):
    return jnp.dot(a, b, preferred_element_type=F32)


def _dot_nt(a, b):
    return lax.dot_general(a, b, (((1,), (1,)), ((), ())), preferred_element_type=F32)


def _rms_scale(v):
    return lax.rsqrt(jnp.mean(v * v, axis=-1, keepdims=True) + NORM_EPS)


def _sub_tiles(rows):
    return [slice(r, r + SUB_ROWS) for r in range(0, rows, SUB_ROWS)]


def _inproj_kernel(x_ref, nw_ref, w_ref, a_ref, b_ref, o_ref, hn_ref):
    j = pl.program_id(1)

    def body(first):
        for rows in _sub_tiles(x_ref.shape[0]):
            if first:
                x = x_ref[rows, :]
                hn = (x * _rms_scale(x) * nw_ref[...]).astype(BF16)
                hn_ref[rows, :] = hn
            else:
                hn = hn_ref[rows, :]
            acc = _dot(hn, w_ref[...])
            a = a_ref[rows, :]
            b = b_ref[rows, :]
            for h in range(N_HEADS):
                cols = slice(h * HEAD_DIM, (h + 1) * HEAD_DIM)
                t = acc[:, cols]
                rot = pltpu.roll(t, HEAD_DIM // 2, axis=1)
                o_ref[rows, cols] = (t * a + rot * b).astype(BF16)

    pl.when(j == 0)(lambda: body(True))
    pl.when(j > 0)(lambda: body(False))


def _inproj(x2, norm_w, w_in_bf16, tab_a, tab_b, seq, tm):
    tokens, d_model = x2.shape
    in_cols = w_in_bf16.shape[1]
    n_groups = in_cols // GROUP_WIDTH
    tiles_per_seq = seq // tm

    def table(i, j):
        rotary = (j == 1) | (j == 3) | (j == 4)
        kind = jnp.where(j == 0, 0, jnp.where(rotary, 1, 2))
        return (kind, i % tiles_per_seq, 0)

    return pl.pallas_call(
        _inproj_kernel,
        grid=(tokens // tm, n_groups),
        in_specs=[
            pl.BlockSpec((tm, d_model), lambda i, j: (i, 0)),
            pl.BlockSpec((1, d_model), lambda i, j: (0, 0)),
            pl.BlockSpec((d_model, GROUP_WIDTH), lambda i, j: (0, j)),
            pl.BlockSpec((None, tm, HEAD_DIM), table),
            pl.BlockSpec((None, tm, HEAD_DIM), table),
        ],
        out_specs=pl.BlockSpec((tm, GROUP_WIDTH), lambda i, j: (i, j)),
        out_shape=jax.ShapeDtypeStruct((tokens, in_cols), BF16),
        scratch_shapes=[pltpu.VMEM((tm, d_model), BF16)],
        compiler_params=pltpu.CompilerParams(
            dimension_semantics=("arbitrary", "arbitrary"),
            vmem_limit_bytes=V7X_VMEM_LIMIT_BYTES),
        name="inproj",
    )(x2, norm_w, w_in_bf16, tab_a, tab_b)


def _moba_kernel(q_ref, k_ref, v_ref, *rest):
    n_cast = (len(rest) - 7) // 2
    cast_in, o_ref, cast_out = rest[:n_cast], rest[n_cast], rest[n_cast + 1:2 * n_cast + 1]
    kaug_ref, vt_ref, kmean_ref, qt_ref, s_ref, p_ref = rest[2 * n_cast + 1:]
    seq = q_ref.shape[0]
    nb = seq // MOBA_BLOCK

    for src, dst in zip(cast_in, cast_out):
        if len(dst.shape) == 2:
            dst[...] = src[...].astype(BF16)
        else:
            width = dst.shape[2]
            for c in range(dst.shape[0]):
                dst[c] = src[:, c * width:(c + 1) * width].astype(BF16)

    lane = lax.broadcasted_iota(jnp.int32, (MOBA_BLOCK, HEAD_DIM), 1)
    for jb in range(nb):
        rows = slice(jb * MOBA_BLOCK, (jb + 1) * MOBA_BLOCK)
        kb = k_ref[rows, :]
        kaug_ref[rows, :HEAD_DIM] = kb
        kaug_ref[rows, HEAD_DIM:] = jnp.where(lane == jb, 1.0, 0.0).astype(BF16)
        vt_ref[:, rows] = v_ref[rows, :].astype(F32).T.astype(BF16)
        kmean_ref[jb:jb + 1, :] = jnp.mean(kb.astype(F32), axis=0, keepdims=True)
    for slot in range(2):
        qt_ref[slot, HEAD_DIM:, :] = jnp.zeros((HEAD_DIM, MOBA_BLOCK), BF16)
    kmean = kmean_ref[...].astype(BF16)

    blk_idx = lax.broadcasted_iota(jnp.int32, (nb, MOBA_BLOCK), 0)
    key_pos = lax.broadcasted_iota(jnp.int32, (MOBA_BLOCK, MOBA_BLOCK), 0)
    qry_pos = lax.broadcasted_iota(jnp.int32, (MOBA_BLOCK, MOBA_BLOCK), 1)
    causal = key_pos <= qry_pos

    col_max = {}

    def score_stage(i):
        slot = i % 2
        qt = q_ref[i * MOBA_BLOCK:(i + 1) * MOBA_BLOCK, :].astype(F32).T.astype(BF16)
        past = blk_idx < i
        gate = jnp.where(past, _dot(kmean, qt), -jnp.inf)
        beaten_by = jnp.zeros((nb, MOBA_BLOCK), jnp.int32)
        for jp in range(i):
            row = gate[jp:jp + 1, :]
            beats = (row > gate) | ((row == gate) & (jp < blk_idx))
            beaten_by = beaten_by + beats.astype(jnp.int32)
        visible = (past & (beaten_by < MOBA_TOPK)) | (blk_idx == i)
        qt_ref[slot, :HEAD_DIM, :] = qt
        qt_ref[slot, HEAD_DIM:HEAD_DIM + nb, :] = jnp.where(visible, 0.0, MASKED_SCORE).astype(BF16)
        qt_aug = qt_ref[slot]
        yield
        m = None
        for t in range(i + 1):
            k_rows = slice(t * MOBA_BLOCK, (t + 1) * MOBA_BLOCK)
            s = _dot(kaug_ref[k_rows, :], qt_aug)
            if t == i:
                s = jnp.where(causal, s, MASKED_SCORE)
            s_ref[slot, k_rows, :] = s
            tile_max = jnp.max(s, axis=0, keepdims=True)
            m = tile_max if m is None else jnp.maximum(m, tile_max)
            yield
        col_max[i] = m

    def value_stage(i):
        slot = i % 2
        m = col_max.pop(i)
        l = None
        for t in range(i + 1):
            k_rows = slice(t * MOBA_BLOCK, (t + 1) * MOBA_BLOCK)
            p = jnp.exp2(s_ref[slot, k_rows, :] - m)
            tile_sum = jnp.sum(p, axis=0, keepdims=True)
            l = tile_sum if l is None else l + tile_sum
            p_ref[slot, k_rows, :] = p.astype(BF16)
            yield
        n_keys = (i + 1) * MOBA_BLOCK
        out_t = _dot(vt_ref[:, :n_keys], p_ref[slot, :n_keys, :]) / l
        o_ref[i * MOBA_BLOCK:(i + 1) * MOBA_BLOCK, :] = out_t.T.astype(BF16)
        yield

    for _ in score_stage(0):
        pass
    for i in range(nb):
        stages = [value_stage(i)] + ([score_stage(i + 1)] if i + 1 < nb else [])
        while stages:
            for stage in list(stages):
                if next(stage, "done") == "done":
                    stages.remove(stage)


def _moba(proj3, weights_f32):
    batch, seq, _ = proj3.shape
    nb = seq // MOBA_BLOCK
    assert nb <= HEAD_DIM, "block-selection rows must fit in the spare contraction rows"
    n_steps = batch * N_HEADS

    def col(group):
        return pl.BlockSpec((None, seq, HEAD_DIM), lambda b, h: (b, 0, group * N_HEADS + h))

    cast_in, cast_out, cast_shapes = [], [], []
    for w, col_chunk in weights_f32:
        n_rows, n_cols = w.shape
        rows = n_rows // n_steps
        assert rows * n_steps == n_rows and rows % BF16_SUBLANES == 0
        cast_in.append(pl.BlockSpec((rows, n_cols), lambda b, h: (b * N_HEADS + h, 0)))
        if col_chunk is None:
            cast_out.append(cast_in[-1])
            cast_shapes.append(jax.ShapeDtypeStruct(w.shape, BF16))
        else:
            n_chunks = n_cols // col_chunk
            cast_out.append(pl.BlockSpec((n_chunks, rows, col_chunk),
                                         lambda b, h: (0, b * N_HEADS + h, 0)))
            cast_shapes.append(jax.ShapeDtypeStruct((n_chunks, n_rows, col_chunk), BF16))

    outs = pl.pallas_call(
        _moba_kernel,
        grid=(batch, N_HEADS),
        in_specs=[col(0), col(1), col(2)] + cast_in,
        out_specs=[pl.BlockSpec((None, seq, HEAD_DIM), lambda b, h: (b, 0, h))] + cast_out,
        out_shape=[jax.ShapeDtypeStruct((batch, seq, GROUP_WIDTH), BF16)] + cast_shapes,
        scratch_shapes=[
            pltpu.VMEM((seq, 2 * HEAD_DIM), BF16),
            pltpu.VMEM((HEAD_DIM, seq), BF16),
            pltpu.VMEM((nb, HEAD_DIM), F32),
            pltpu.VMEM((2, 2 * HEAD_DIM, MOBA_BLOCK), BF16),
            pltpu.VMEM((2, seq, MOBA_BLOCK), F32),
            pltpu.VMEM((2, seq, MOBA_BLOCK), BF16),
        ],
        compiler_params=pltpu.CompilerParams(
            dimension_semantics=("arbitrary", "arbitrary"),
            vmem_limit_bytes=V7X_VMEM_LIMIT_BYTES),
        name="moba",
    )(proj3, proj3, proj3, *[w for w, _ in weights_f32])
    return outs[0], outs[1:]


def _retention_kernel(logg_ref, q_ref, k_ref, v_ref, g_ref, gw_ref, gb_ref, o_ref):
    seq = q_ref.shape[0]
    log_g = logg_ref[pl.program_id(1)]

    row = lax.broadcasted_iota(jnp.int32, (RET_CHUNK, RET_CHUNK), 0)
    colm = lax.broadcasted_iota(jnp.int32, (RET_CHUNK, RET_CHUNK), 1)
    rel = (row - colm).astype(F32)
    decay = jnp.where(rel >= 0, jnp.exp(log_g * jnp.maximum(rel, 0.0)), 0.0) * ATTN_SCALE
    pos = lax.broadcasted_iota(jnp.int32, (RET_CHUNK, HEAD_DIM), 0).astype(F32)
    xi = jnp.exp(log_g * (pos + 1.0))
    zeta = jnp.exp(log_g * (RET_CHUNK - 1.0 - pos)) * ATTN_SCALE
    g_chunk = jnp.exp(log_g * RET_CHUNK)
    gn_w = gw_ref[...]
    gn_b = gb_ref[...]

    state = jnp.zeros((HEAD_DIM, HEAD_DIM), F32)
    for c in range(seq // RET_CHUNK):
        rows = slice(c * RET_CHUNK, (c + 1) * RET_CHUNK)
        qc = q_ref[rows, :]
        kc = k_ref[rows, :]
        vc = v_ref[rows, :]
        inner = _dot_nt(qc, kc) * decay
        y = _dot(inner.astype(BF16), vc)
        if c > 0:
            y = y + _dot(qc, state.astype(BF16)) * xi
        kz_t = (kc.astype(F32) * zeta).T.astype(BF16)
        state = state * g_chunk + _dot(kz_t, vc)
        mu = jnp.mean(y, axis=-1, keepdims=True)
        yc = y - mu
        var = jnp.mean(yc * yc, axis=-1, keepdims=True)
        r = yc * lax.rsqrt(var + GN_EPS) * gn_w + gn_b
        g = g_ref[rows, :].astype(F32)
        o_ref[rows, :] = (g * jax.nn.sigmoid(g) * r).astype(BF16)


def _retention(proj3, log_g, gn_w, gn_b):
    batch, seq, _ = proj3.shape

    def col(group):
        return pl.BlockSpec((None, seq, HEAD_DIM), lambda b, h, lg: (b, 0, group * N_HEADS + h))

    head_vec = pl.BlockSpec((1, HEAD_DIM), lambda b, h, lg: (0, h))
    return pl.pallas_call(
        _retention_kernel,
        grid_spec=pltpu.PrefetchScalarGridSpec(
            num_scalar_prefetch=1,
            grid=(batch, N_HEADS),
            in_specs=[col(3), col(4), col(5), col(6), head_vec, head_vec],
            out_specs=pl.BlockSpec((None, seq, HEAD_DIM), lambda b, h, lg: (b, 0, h)),
        ),
        out_shape=jax.ShapeDtypeStruct((batch, seq, GROUP_WIDTH), BF16),
        compiler_params=pltpu.CompilerParams(
            dimension_semantics=("arbitrary", "arbitrary"),
            vmem_limit_bytes=V7X_VMEM_LIMIT_BYTES),
        name="retention",
    )(log_g, proj3, proj3, proj3, proj3, gn_w, gn_b)


def _outproj_kernel(a_ref, r_ref, w_ref, x_ref, post_ref, pre_ref, x1_ref, hn_ref):
    for rows in _sub_tiles(x_ref.shape[0]):
        mix = (_dot(a_ref[rows, :], w_ref[:GROUP_WIDTH, :])
               + _dot(r_ref[rows, :], w_ref[GROUP_WIDTH:, :]))
        x1 = x_ref[rows, :] + mix * _rms_scale(mix) * post_ref[...]
        x1_ref[rows, :] = x1
        hn_ref[rows, :] = (x1 * _rms_scale(x1) * pre_ref[...]).astype(BF16)


def _outproj(a2, r2, w_out_bf16, x2, norm_post, norm_pre, tm):
    tokens, d_model = x2.shape
    row_tile = lambda width: pl.BlockSpec((tm, width), lambda i: (i, 0))
    whole = lambda shape: pl.BlockSpec(shape, lambda i: (0, 0))
    return pl.pallas_call(
        _outproj_kernel,
        grid=(tokens // tm,),
        in_specs=[row_tile(GROUP_WIDTH), row_tile(GROUP_WIDTH), whole(w_out_bf16.shape),
                  row_tile(d_model), whole((1, d_model)), whole((1, d_model))],
        out_specs=[row_tile(d_model), row_tile(d_model)],
        out_shape=[jax.ShapeDtypeStruct((tokens, d_model), F32),
                   jax.ShapeDtypeStruct((tokens, d_model), BF16)],
        compiler_params=pltpu.CompilerParams(
            dimension_semantics=("arbitrary",),
            vmem_limit_bytes=V7X_VMEM_LIMIT_BYTES),
        name="outproj",
    )(a2, r2, w_out_bf16, x2, norm_post, norm_pre)


def _ffn_kernel(hn_ref, wg_ref, wv_ref, cw_ref, cb_ref, wd_ref, x1c_ref, post_ref,
                o_ref, x1_ref, carry_ref, *, tiles_per_seq):
    i = pl.program_id(0)
    j = pl.program_id(1)
    last = pl.num_programs(1) - 1
    tf = wg_ref.shape[1]
    n_x1, _, x1_cols = x1_ref.shape

    @pl.when(i % tiles_per_seq == 0)
    def _():
        carry_ref[j] = jnp.zeros((8, tf), F32)

    @pl.when(j < n_x1)
    def _():
        x1_ref[j] = x1c_ref[...]

    def body(first_chunk, last_chunk):
        cw = cw_ref[...]
        cb = cb_ref[...]
        history = carry_ref[j]
        prev2, prev1 = history[6:7, :], history[7:8, :]
        row = lax.broadcasted_iota(jnp.int32, (SUB_ROWS, tf), 0)
        for rows in _sub_tiles(hn_ref.shape[0]):
            hn = hn_ref[rows, :]
            ug = _dot(hn, wg_ref[...])
            uv = _dot(hn, wv_ref[...])
            back1 = jnp.where(row == 0, prev1, pltpu.roll(ug, 1, axis=0))
            back2 = jnp.where(row == 0, prev2,
                              jnp.where(row == 1, prev1, pltpu.roll(ug, 2, axis=0)))
            conv = cw[0:1, :] * back2 + cw[1:2, :] * back1 + cw[2:3, :] * ug + cb
            hidden = (conv * jax.nn.sigmoid(conv) * uv).astype(BF16)
            part = _dot(hidden, wd_ref[...])
            ff = part if first_chunk else o_ref[rows, :] + part
            if last_chunk:
                normed = ff * _rms_scale(ff) * post_ref[...]
                for c in range(n_x1):
                    cols = slice(c * x1_cols, (c + 1) * x1_cols)
                    o_ref[rows, cols] = x1_ref[c, rows, :] + normed[:, cols]
            else:
                o_ref[rows, :] = ff
            prev2, prev1 = ug[SUB_ROWS - 2:SUB_ROWS - 1, :], ug[SUB_ROWS - 1:, :]
            tail = ug[SUB_ROWS - 8:, :]
        carry_ref[j] = tail

    pl.when(j == 0)(lambda: body(True, False))
    pl.when((j > 0) & (j < last))(lambda: body(False, False))
    pl.when(j == last)(lambda: body(False, True))


def _ffn(hn2, w_up_bf16, conv_w, conv_b, w_down_bf16, x1, norm_post, seq, tm, tf):
    tokens, d_model = x1.shape
    d_ff = w_down_bf16.shape[0]
    n_chunks = d_ff // tf
    n_x1 = d_model // X1_SLAB_COLS
    assert n_chunks >= 2 and n_chunks >= n_x1
    row_tile = pl.BlockSpec((tm, d_model), lambda i, j: (i, 0))
    return pl.pallas_call(
        functools.partial(_ffn_kernel, tiles_per_seq=seq // tm),
        grid=(tokens // tm, n_chunks),
        in_specs=[
            row_tile,
            pl.BlockSpec((None, d_model, tf), lambda i, j: (j, 0, 0)),
            pl.BlockSpec((None, d_model, tf), lambda i, j: (n_chunks + j, 0, 0)),
            pl.BlockSpec((CONV_WIDTH, tf), lambda i, j: (0, j)),
            pl.BlockSpec((1, tf), lambda i, j: (0, j)),
            pl.BlockSpec((tf, d_model), lambda i, j: (j, 0)),
            pl.BlockSpec((tm, X1_SLAB_COLS), lambda i, j: (i, jnp.minimum(j, n_x1 - 1))),
            pl.BlockSpec((1, d_model), lambda i, j: (0, 0)),
        ],
        out_specs=row_tile,
        out_shape=jax.ShapeDtypeStruct((tokens, d_model), F32),
        scratch_shapes=[pltpu.VMEM((n_x1, tm, X1_SLAB_COLS), F32),
                        pltpu.VMEM((n_chunks, 8, tf), F32)],
        compiler_params=pltpu.CompilerParams(
            dimension_semantics=("arbitrary", "arbitrary"),
            vmem_limit_bytes=V7X_VMEM_LIMIT_BYTES),
        name="ffn",
    )(hn2, w_up_bf16, w_up_bf16, conv_w, conv_b, w_down_bf16, x1, norm_post)


def _rope_tables(seq):
    pos = jnp.arange(seq, dtype=F32)
    inv = 1.0 / (ROPE_THETA ** (jnp.arange(0, HEAD_DIM, 2, dtype=F32) / HEAD_DIM))
    ang = pos[:, None] * inv[None, :]
    cos, sin = jnp.cos(ang), jnp.sin(ang)
    cos_full = jnp.concatenate([cos, cos], axis=-1)
    sin_signed = jnp.concatenate([-sin, sin], axis=-1)
    tab_a = jnp.stack([cos_full * MOBA_Q_SCALE, cos_full, jnp.ones_like(cos_full)])
    tab_b = jnp.stack([sin_signed * MOBA_Q_SCALE, sin_signed, jnp.zeros_like(sin_signed)])
    return tab_a, tab_b


def _tile(n, preferred):
    for t in preferred:
        if n % t == 0:
            return t
    raise ValueError(f"no supported tile for extent {n}")


def kernel(x, norm_mix_pre, w_in, ret_gn_w, ret_gn_b, w_out, norm_mix_post, norm_ffn_pre, w_up, conv_w, conv_b, w_down, norm_ffn_post):
    batch, seq, d_model = x.shape
    depth = w_in.shape[0]
    assert seq % MOBA_BLOCK == 0 and seq % RET_CHUNK == 0
    assert w_in.shape[2] == 7 * GROUP_WIDTH and w_out.shape[1] == 2 * GROUP_WIDTH
    d_ff = w_down.shape[1]
    tokens = batch * seq
    tm_proj = _tile(seq, (1024, 512, 256))
    tm_out = _tile(seq, (512, 256))
    tm_ffn = _tile(seq, (1024, 512, 256))
    tf = _tile(d_ff, (512, 256, 128))

    tab_a, tab_b = _rope_tables(seq)
    log_g = jnp.log(1.0 - 2.0 ** (-5.0 - jnp.arange(N_HEADS, dtype=F32)))

    x2 = x.reshape(tokens, d_model)
    for layer in range(depth):
        proj = _inproj(x2, norm_mix_pre[layer][None, :], w_in[layer].astype(BF16),
                       tab_a, tab_b, seq, tm_proj)
        proj3 = proj.reshape(batch, seq, -1)
        a, (w_out_bf16, w_up_bf16, w_down_bf16) = _moba(
            proj3, ((w_out[layer], None), (w_up[layer], tf), (w_down[layer], None)))
        r = _retention(proj3, log_g, ret_gn_w[layer][None, :], ret_gn_b[layer][None, :])
        x1, hn2 = _outproj(a.reshape(tokens, GROUP_WIDTH), r.reshape(tokens, GROUP_WIDTH),
                           w_out_bf16, x2,
                           norm_mix_post[layer][None, :], norm_ffn_pre[layer][None, :], tm_out)
        x2 = _ffn(hn2, w_up_bf16, conv_w[layer], conv_b[layer][None, :],
                  w_down_bf16, x1, norm_ffn_post[layer][None, :], seq, tm_ffn, tf)
    return x2.reshape(batch, seq, d_model)
```

```python
import functools

import jax
import jax.numpy as jnp
import numpy as np
from jax import lax
from jax.experimental import pallas as pl
from jax.experimental.pallas import tpu as pltpu

HEAD_DIM = 128
N_HEADS = 8
GROUP_WIDTH = N_HEADS * HEAD_DIM
MOBA_BLOCK = 256
MOBA_TOPK = 3
RET_CHUNK = 256
ROPE_THETA = 10000.0
CONV_WIDTH = 3
NORM_EPS = 1e-6
GN_EPS = 1e-5
ATTN_SCALE = HEAD_DIM ** -0.5
MASKED_SCORE = -1e30
MOBA_Q_SCALE = ATTN_SCALE * 1.4426950408889634

V7X_VMEM_LIMIT_BYTES = 56 * 1024 * 1024
SUB_ROWS = 256
X1_SLAB_COLS = 256
BF16_SUBLANES = 16

F32 = jnp.float32
BF16 = jnp.bfloat16


def _dot(a, b):
    return jnp.dot(a, b, preferred_element_type=F32)


def _dot_nt(a, b):
    return lax.dot_general(a, b, (((1,), (1,)), ((), ())), preferred_element_type=F32)


def _rms_scale(v):
    return lax.rsqrt(jnp.mean(v * v, axis=-1, keepdims=True) + NORM_EPS)


def _sub_tiles(rows):
    return [slice(r, r + SUB_ROWS) for r in range(0, rows, SUB_ROWS)]


def _inproj_kernel(x_ref, nw_ref, w_ref, a_ref, b_ref, o_ref, hn_ref):
    j = pl.program_id(1)

    def body(first):
        for rows in _sub_tiles(x_ref.shape[0]):
            if first:
                x = x_ref[rows, :]
                hn = (x * _rms_scale(x) * nw_ref[...]).astype(BF16)
                hn_ref[rows, :] = hn
            else:
                hn = hn_ref[rows, :]
            acc = _dot(hn, w_ref[...])
            a = a_ref[rows, :]
            b = b_ref[rows, :]
            for h in range(N_HEADS):
                cols = slice(h * HEAD_DIM, (h + 1) * HEAD_DIM)
                t = acc[:, cols]
                rot = pltpu.roll(t, HEAD_DIM // 2, axis=1)
                o_ref[rows, cols] = (t * a + rot * b).astype(BF16)

    pl.when(j == 0)(lambda: body(True))
    pl.when(j > 0)(lambda: body(False))


def _inproj(x2, norm_w, w_in_bf16, tab_a, tab_b, seq, tm):
    tokens, d_model = x2.shape
    in_cols = w_in_bf16.shape[1]
    n_groups = in_cols // GROUP_WIDTH
    tiles_per_seq = seq // tm

    def table(i, j):
        rotary = (j == 1) | (j == 3) | (j == 4)
        kind = jnp.where(j == 0, 0, jnp.where(rotary, 1, 2))
        return (kind, i % tiles_per_seq, 0)

    return pl.pallas_call(
        _inproj_kernel,
        grid=(tokens // tm, n_groups),
        in_specs=[
            pl.BlockSpec((tm, d_model), lambda i, j: (i, 0)),
            pl.BlockSpec((1, d_model), lambda i, j: (0, 0)),
            pl.BlockSpec((d_model, GROUP_WIDTH), lambda i, j: (0, j)),
            pl.BlockSpec((None, tm, HEAD_DIM), table),
            pl.BlockSpec((None, tm, HEAD_DIM), table),
        ],
        out_specs=pl.BlockSpec((tm, GROUP_WIDTH), lambda i, j: (i, j)),
        out_shape=jax.ShapeDtypeStruct((tokens, in_cols), BF16),
        scratch_shapes=[pltpu.VMEM((tm, d_model), BF16)],
        compiler_params=pltpu.CompilerParams(
            dimension_semantics=("arbitrary", "arbitrary"),
            vmem_limit_bytes=V7X_VMEM_LIMIT_BYTES),
        name="inproj",
    )(x2, norm_w, w_in_bf16, tab_a, tab_b)


def _mixer_kernel(logg_ref, q_ref, k_ref, v_ref, qr_ref, kr_ref, vr_ref, g_ref, gw_ref, gb_ref, *rest):
    n_cast = (len(rest) - 8) // 2
    cast_in, a_ref, r_ref = rest[:n_cast], rest[n_cast], rest[n_cast + 1]
    cast_out = rest[n_cast + 2:2 * n_cast + 2]
    kaug_ref, vt_ref, kmean_ref, qt_ref, s_ref, p_ref = rest[2 * n_cast + 2:]
    seq = q_ref.shape[0]
    nb = seq // MOBA_BLOCK

    for src, dst in zip(cast_in, cast_out):
        if len(dst.shape) == 2:
            dst[...] = src[...].astype(BF16)
        else:
            width = dst.shape[2]
            for c in range(dst.shape[0]):
                dst[c] = src[:, c * width:(c + 1) * width].astype(BF16)

    lane = lax.broadcasted_iota(jnp.int32, (MOBA_BLOCK, HEAD_DIM), 1)
    for jb in range(nb):
        rows = slice(jb * MOBA_BLOCK, (jb + 1) * MOBA_BLOCK)
        kb = k_ref[rows, :]
        kaug_ref[rows, :HEAD_DIM] = kb
        kaug_ref[rows, HEAD_DIM:] = jnp.where(lane == jb, 1.0, 0.0).astype(BF16)
        vt_ref[:, rows] = v_ref[rows, :].astype(F32).T.astype(BF16)
        kmean_ref[jb:jb + 1, :] = jnp.mean(kb.astype(F32), axis=0, keepdims=True)
    for slot in range(2):
        qt_ref[slot, HEAD_DIM:, :] = jnp.zeros((HEAD_DIM, MOBA_BLOCK), BF16)
    kmean = kmean_ref[...].astype(BF16)

    blk_idx = lax.broadcasted_iota(jnp.int32, (nb, MOBA_BLOCK), 0)
    key_pos = lax.broadcasted_iota(jnp.int32, (MOBA_BLOCK, MOBA_BLOCK), 0)
    qry_pos = lax.broadcasted_iota(jnp.int32, (MOBA_BLOCK, MOBA_BLOCK), 1)
    causal = key_pos <= qry_pos

    col_max = {}

    def score_stage(i):
        slot = i % 2
        qt = q_ref[i * MOBA_BLOCK:(i + 1) * MOBA_BLOCK, :].astype(F32).T.astype(BF16)
        past = blk_idx < i
        gate = jnp.where(past, _dot(kmean, qt), -jnp.inf)
        beaten_by = jnp.zeros((nb, MOBA_BLOCK), jnp.int32)
        for jp in range(i):
            row = gate[jp:jp + 1, :]
            beats = (row > gate) | ((row == gate) & (jp < blk_idx))
            beaten_by = beaten_by + beats.astype(jnp.int32)
        visible = (past & (beaten_by < MOBA_TOPK)) | (blk_idx == i)
        qt_ref[slot, :HEAD_DIM, :] = qt
        qt_ref[slot, HEAD_DIM:HEAD_DIM + nb, :] = jnp.where(visible, 0.0, MASKED_SCORE).astype(BF16)
        qt_aug = qt_ref[slot]
        yield
        m = None
        for t in range(i + 1):
            k_rows = slice(t * MOBA_BLOCK, (t + 1) * MOBA_BLOCK)
            s = _dot(kaug_ref[k_rows, :], qt_aug)
            if t == i:
                s = jnp.where(causal, s, MASKED_SCORE)
            s_ref[slot, k_rows, :] = s
            tile_max = jnp.max(s, axis=0, keepdims=True)
            m = tile_max if m is None else jnp.maximum(m, tile_max)
            yield
        col_max[i] = m

    def value_stage(i):
        slot = i % 2
        m = col_max.pop(i)
        l = None
        for t in range(i + 1):
            k_rows = slice(t * MOBA_BLOCK, (t + 1) * MOBA_BLOCK)
            p = jnp.exp2(s_ref[slot, k_rows, :] - m)
            tile_sum = jnp.sum(p, axis=0, keepdims=True)
            l = tile_sum if l is None else l + tile_sum
            p_ref[slot, k_rows, :] = p.astype(BF16)
            yield
        n_keys = (i + 1) * MOBA_BLOCK
        out_t = _dot(vt_ref[:, :n_keys], p_ref[slot, :n_keys, :]) / l
        a_ref[i * MOBA_BLOCK:(i + 1) * MOBA_BLOCK, :] = out_t.T.astype(BF16)
        yield

    def retention_stage():
        log_g = logg_ref[pl.program_id(1)]
        row = lax.broadcasted_iota(jnp.int32, (RET_CHUNK, RET_CHUNK), 0)
        colm = lax.broadcasted_iota(jnp.int32, (RET_CHUNK, RET_CHUNK), 1)
        rel = (row - colm).astype(F32)
        decay = jnp.where(rel >= 0, jnp.exp(log_g * jnp.maximum(rel, 0.0)), 0.0) * ATTN_SCALE
        pos = lax.broadcasted_iota(jnp.int32, (RET_CHUNK, HEAD_DIM), 0).astype(F32)
        xi = jnp.exp(log_g * (pos + 1.0))
        zeta = jnp.exp(log_g * (RET_CHUNK - 1.0 - pos)) * ATTN_SCALE
        g_chunk = jnp.exp(log_g * RET_CHUNK)
        gn_w = gw_ref[...]
        gn_b = gb_ref[...]
        state = jnp.zeros((HEAD_DIM, HEAD_DIM), F32)
        for c in range(seq // RET_CHUNK):
            rows = slice(c * RET_CHUNK, (c + 1) * RET_CHUNK)
            qc = qr_ref[rows, :]
            kc = kr_ref[rows, :]
            vc = vr_ref[rows, :]
            inner = _dot_nt(qc, kc) * decay
            y = _dot(inner.astype(BF16), vc)
            if c > 0:
                y = y + _dot(qc, state.astype(BF16)) * xi
            kz_t = (kc.astype(F32) * zeta).T.astype(BF16)
            state = state * g_chunk + _dot(kz_t, vc)
            mu = jnp.mean(y, axis=-1, keepdims=True)
            yc = y - mu
            var = jnp.mean(yc * yc, axis=-1, keepdims=True)
            r = yc * lax.rsqrt(var + GN_EPS) * gn_w + gn_b
            g = g_ref[rows, :].astype(F32)
            r_ref[rows, :] = (g * jax.nn.sigmoid(g) * r).astype(BF16)
            yield

    retention = retention_stage()
    for _ in score_stage(0):
        pass
    for i in range(nb):
        stages = [value_stage(i)] + ([score_stage(i + 1)] if i + 1 < nb else [])
        first_pass = True
        while stages:
            for stage in list(stages):
                if next(stage, "done") == "done":
                    stages.remove(stage)
            if first_pass:
                next(retention, None)
                first_pass = False
    for _ in retention:
        pass


def _mixer(proj3, log_g, gn_w, gn_b, weights_f32):
    batch, seq, _ = proj3.shape
    nb = seq // MOBA_BLOCK
    assert nb <= HEAD_DIM, "block-selection rows must fit in the spare contraction rows"
    n_steps = batch * N_HEADS

    def col(group):
        return pl.BlockSpec((None, seq, HEAD_DIM), lambda b, h, lg: (b, 0, group * N_HEADS + h))

    head_vec = pl.BlockSpec((1, HEAD_DIM), lambda b, h, lg: (0, h))
    head_out = pl.BlockSpec((None, seq, HEAD_DIM), lambda b, h, lg: (b, 0, h))

    cast_in, cast_out, cast_shapes = [], [], []
    for w, col_chunk in weights_f32:
        n_rows, n_cols = w.shape
        rows = n_rows // n_steps
        assert rows * n_steps == n_rows and rows % BF16_SUBLANES == 0
        cast_in.append(pl.BlockSpec((rows, n_cols), lambda b, h, lg: (b * N_HEADS + h, 0)))
        if col_chunk is None:
            cast_out.append(cast_in[-1])
            cast_shapes.append(jax.ShapeDtypeStruct(w.shape, BF16))
        else:
            n_chunks = n_cols // col_chunk
            cast_out.append(pl.BlockSpec((n_chunks, rows, col_chunk),
                                         lambda b, h, lg: (0, b * N_HEADS + h, 0)))
            cast_shapes.append(jax.ShapeDtypeStruct((n_chunks, n_rows, col_chunk), BF16))

    head_shape = jax.ShapeDtypeStruct((batch, seq, GROUP_WIDTH), BF16)
    outs = pl.pallas_call(
        _mixer_kernel,
        grid_spec=pltpu.PrefetchScalarGridSpec(
            num_scalar_prefetch=1,
            grid=(batch, N_HEADS),
            in_specs=[col(0), col(1), col(2), col(3), col(4), col(5), col(6), head_vec, head_vec]
            + cast_in,
            out_specs=[head_out, head_out] + cast_out,
            scratch_shapes=[
                pltpu.VMEM((seq, 2 * HEAD_DIM), BF16),
                pltpu.VMEM((HEAD_DIM, seq), BF16),
                pltpu.VMEM((nb, HEAD_DIM), F32),
                pltpu.VMEM((2, 2 * HEAD_DIM, MOBA_BLOCK), BF16),
                pltpu.VMEM((2, seq, MOBA_BLOCK), F32),
                pltpu.VMEM((2, seq, MOBA_BLOCK), BF16),
            ],
        ),
        out_shape=[head_shape, head_shape] + cast_shapes,
        compiler_params=pltpu.CompilerParams(
            dimension_semantics=("arbitrary", "arbitrary"),
            vmem_limit_bytes=V7X_VMEM_LIMIT_BYTES),
        name="mixer",
    )(log_g, *([proj3] * 7), gn_w, gn_b, *[w for w, _ in weights_f32])
    return outs[0], outs[1], outs[2:]


def _outproj_kernel(a_ref, r_ref, w_ref, x_ref, post_ref, pre_ref, x1_ref, hn_ref):
    for rows in _sub_tiles(x_ref.shape[0]):
        mix = (_dot(a_ref[rows, :], w_ref[:GROUP_WIDTH, :])
               + _dot(r_ref[rows, :], w_ref[GROUP_WIDTH:, :]))
        x1 = x_ref[rows, :] + mix * _rms_scale(mix) * post_ref[...]
        x1_ref[rows, :] = x1
        hn_ref[rows, :] = (x1 * _rms_scale(x1) * pre_ref[...]).astype(BF16)


def _outproj(a2, r2, w_out_bf16, x2, norm_post, norm_pre, tm):
    tokens, d_model = x2.shape
    row_tile = lambda width: pl.BlockSpec((tm, width), lambda i: (i, 0))
    whole = lambda shape: pl.BlockSpec(shape, lambda i: (0, 0))
    return pl.pallas_call(
        _outproj_kernel,
        grid=(tokens // tm,),
        in_specs=[row_tile(GROUP_WIDTH), row_tile(GROUP_WIDTH), whole(w_out_bf16.shape),
                  row_tile(d_model), whole((1, d_model)), whole((1, d_model))],
        out_specs=[row_tile(d_model), row_tile(d_model)],
        out_shape=[jax.ShapeDtypeStruct((tokens, d_model), F32),
                   jax.ShapeDtypeStruct((tokens, d_model), BF16)],
        compiler_params=pltpu.CompilerParams(
            dimension_semantics=("arbitrary",),
            vmem_limit_bytes=V7X_VMEM_LIMIT_BYTES),
        name="outproj",
    )(a2, r2, w_out_bf16, x2, norm_post, norm_pre)


def _ffn_kernel(hn_ref, wg_ref, wv_ref, cw_ref, cb_ref, wd_ref, x1c_ref, post_ref,
                o_ref, x1_ref, carry_ref, *, tiles_per_seq):
    i = pl.program_id(0)
    j = pl.program_id(1)
    last = pl.num_programs(1) - 1
    tf = wg_ref.shape[1]
    n_x1, _, x1_cols = x1_ref.shape

    @pl.when(i % tiles_per_seq == 0)
    def _():
        carry_ref[j] = jnp.zeros((8, tf), F32)

    @pl.when(j < n_x1)
    def _():
        x1_ref[j] = x1c_ref[...]

    def body(first_chunk, last_chunk):
        cw = cw_ref[...]
        cb = cb_ref[...]
        history = carry_ref[j]
        prev2, prev1 = history[6:7, :], history[7:8, :]
        row = lax.broadcasted_iota(jnp.int32, (SUB_ROWS, tf), 0)
        for rows in _sub_tiles(hn_ref.shape[0]):
            hn = hn_ref[rows, :]
            ug = _dot(hn, wg_ref[...])
            uv = _dot(hn, wv_ref[...])
            back1 = jnp.where(row == 0, prev1, pltpu.roll(ug, 1, axis=0))
            back2 = jnp.where(row == 0, prev2,
                              jnp.where(row == 1, prev1, pltpu.roll(ug, 2, axis=0)))
            conv = cw[0:1, :] * back2 + cw[1:2, :] * back1 + cw[2:3, :] * ug + cb
            hidden = (conv * jax.nn.sigmoid(conv) * uv).astype(BF16)
            part = _dot(hidden, wd_ref[...])
            ff = part if first_chunk else o_ref[rows, :] + part
            if last_chunk:
                normed = ff * _rms_scale(ff) * post_ref[...]
                for c in range(n_x1):
                    cols = slice(c * x1_cols, (c + 1) * x1_cols)
                    o_ref[rows, cols] = x1_ref[c, rows, :] + normed[:, cols]
            else:
                o_ref[rows, :] = ff
            prev2, prev1 = ug[SUB_ROWS - 2:SUB_ROWS - 1, :], ug[SUB_ROWS - 1:, :]
            tail = ug[SUB_ROWS - 8:, :]
        carry_ref[j] = tail

    pl.when(j == 0)(lambda: body(True, False))
    pl.when((j > 0) & (j < last))(lambda: body(False, False))
    pl.when(j == last)(lambda: body(False, True))


def _ffn(hn2, w_up_bf16, conv_w, conv_b, w_down_bf16, x1, norm_post, seq, tm, tf):
    tokens, d_model = x1.shape
    d_ff = w_down_bf16.shape[0]
    n_chunks = d_ff // tf
    n_x1 = d_model // X1_SLAB_COLS
    assert n_chunks >= 2 and n_chunks >= n_x1
    row_tile = pl.BlockSpec((tm, d_model), lambda i, j: (i, 0))
    return pl.pallas_call(
        functools.partial(_ffn_kernel, tiles_per_seq=seq // tm),
        grid=(tokens // tm, n_chunks),
        in_specs=[
            row_tile,
            pl.BlockSpec((None, d_model, tf), lambda i, j: (j, 0, 0)),
            pl.BlockSpec((None, d_model, tf), lambda i, j: (n_chunks + j, 0, 0)),
            pl.BlockSpec((CONV_WIDTH, tf), lambda i, j: (0, j)),
            pl.BlockSpec((1, tf), lambda i, j: (0, j)),
            pl.BlockSpec((tf, d_model), lambda i, j: (j, 0)),
            pl.BlockSpec((tm, X1_SLAB_COLS), lambda i, j: (i, jnp.minimum(j, n_x1 - 1))),
            pl.BlockSpec((1, d_model), lambda i, j: (0, 0)),
        ],
        out_specs=row_tile,
        out_shape=jax.ShapeDtypeStruct((tokens, d_model), F32),
        scratch_shapes=[pltpu.VMEM((n_x1, tm, X1_SLAB_COLS), F32),
                        pltpu.VMEM((n_chunks, 8, tf), F32)],
        compiler_params=pltpu.CompilerParams(
            dimension_semantics=("arbitrary", "arbitrary"),
            vmem_limit_bytes=V7X_VMEM_LIMIT_BYTES),
        name="ffn",
    )(hn2, w_up_bf16, w_up_bf16, conv_w, conv_b, w_down_bf16, x1, norm_post)


def _rope_tables(seq):
    pos = np.arange(seq, dtype=np.float32)
    inv = (1.0 / (ROPE_THETA ** (np.arange(0, HEAD_DIM, 2, dtype=np.float32) / HEAD_DIM))).astype(np.float32)
    ang = pos[:, None] * inv[None, :]
    cos, sin = np.cos(ang), np.sin(ang)
    cos_full = np.concatenate([cos, cos], axis=-1)
    sin_signed = np.concatenate([-sin, sin], axis=-1)
    q_scale = np.float32(MOBA_Q_SCALE)
    tab_a = np.stack([cos_full * q_scale, cos_full, np.ones_like(cos_full)])
    tab_b = np.stack([sin_signed * q_scale, sin_signed, np.zeros_like(sin_signed)])
    return jnp.asarray(tab_a, F32), jnp.asarray(tab_b, F32)


def _retention_log_decay():
    heads = np.arange(N_HEADS, dtype=np.float32)
    return jnp.asarray(np.log(np.float32(1.0) - np.float32(2.0) ** (-5.0 - heads)), F32)


def _tile(n, preferred):
    for t in preferred:
        if n % t == 0:
            return t
    raise ValueError(f"no supported tile for extent {n}")


def kernel(x, norm_mix_pre, w_in, ret_gn_w, ret_gn_b, w_out, norm_mix_post, norm_ffn_pre, w_up, conv_w, conv_b, w_down, norm_ffn_post):
    batch, seq, d_model = x.shape
    depth = w_in.shape[0]
    assert seq % MOBA_BLOCK == 0 and seq % RET_CHUNK == 0
    assert w_in.shape[2] == 7 * GROUP_WIDTH and w_out.shape[1] == 2 * GROUP_WIDTH
    d_ff = w_down.shape[1]
    tokens = batch * seq
    tm_proj = _tile(seq, (1024, 512, 256))
    tm_out = _tile(seq, (512, 256))
    tm_ffn = _tile(seq, (1024, 512, 256))
    tf = _tile(d_ff, (512, 256, 128))

    tab_a, tab_b = _rope_tables(seq)
    log_g = _retention_log_decay()

    x2 = x.reshape(tokens, d_model)
    for layer in range(depth):
        proj = _inproj(x2, norm_mix_pre[layer][None, :], w_in[layer].astype(BF16),
                       tab_a, tab_b, seq, tm_proj)
        proj3 = proj.reshape(batch, seq, -1)
        a, r, (w_out_bf16, w_up_bf16, w_down_bf16) = _mixer(
            proj3, log_g, ret_gn_w[layer][None, :], ret_gn_b[layer][None, :],
            ((w_out[layer], None), (w_up[layer], tf), (w_down[layer], None)))
        x1, hn2 = _outproj(a.reshape(tokens, GROUP_WIDTH), r.reshape(tokens, GROUP_WIDTH),
                           w_out_bf16, x2,
                           norm_mix_post[layer][None, :], norm_ffn_pre[layer][None, :], tm_out)
        x2 = _ffn(hn2, w_up_bf16, conv_w[layer], conv_b[layer][None, :],
                  w_down_bf16, x1, norm_ffn_post[layer][None, :], seq, tm_ffn, tf)
    return x2.reshape(batch, seq, d_model)
```

```python
import functools

import jax
import jax.numpy as jnp
import numpy as np
from jax import lax
from jax.experimental import pallas as pl
from jax.experimental.pallas import tpu as pltpu

HEAD_DIM = 128
N_HEADS = 8
GROUP_WIDTH = N_HEADS * HEAD_DIM
MOBA_BLOCK = 256
MOBA_TOPK = 3
RET_CHUNK = 256
ROPE_THETA = 10000.0
CONV_WIDTH = 3
NORM_EPS = 1e-6
GN_EPS = 1e-5
ATTN_SCALE = HEAD_DIM ** -0.5
MASKED_SCORE = -1e30
MOBA_Q_SCALE = ATTN_SCALE * 1.4426950408889634

V7X_VMEM_LIMIT_BYTES = 56 * 1024 * 1024
SUB_ROWS = 256
X1_SLAB_COLS = 256
BF16_SUBLANES = 16

F32 = jnp.float32
BF16 = jnp.bfloat16


def _dot(a, b):
    return jnp.dot(a, b, preferred_element_type=F32)


def _dot_nt(a, b):
    return lax.dot_general(a, b, (((1,), (1,)), ((), ())), preferred_element_type=F32)


def _rms_scale(v):
    return lax.rsqrt(jnp.mean(v * v, axis=-1, keepdims=True) + NORM_EPS)


def _sub_tiles(rows):
    return [slice(r, r + SUB_ROWS) for r in range(0, rows, SUB_ROWS)]


def _inproj_kernel(x_ref, nw_ref, w_ref, a_ref, b_ref, o_ref, hn_ref):
    j = pl.program_id(1)

    def body(first):
        for rows in _sub_tiles(x_ref.shape[0]):
            if first:
                x = x_ref[rows, :]
                hn = (x * _rms_scale(x) * nw_ref[...]).astype(BF16)
                hn_ref[rows, :] = hn
            else:
                hn = hn_ref[rows, :]
            acc = _dot(hn, w_ref[...])
            a = a_ref[rows, :]
            b = b_ref[rows, :]
            for h in range(N_HEADS):
                cols = slice(h * HEAD_DIM, (h + 1) * HEAD_DIM)
                t = acc[:, cols]
                rot = pltpu.roll(t, HEAD_DIM // 2, axis=1)
                o_ref[rows, cols] = (t * a + rot * b).astype(BF16)

    pl.when(j == 0)(lambda: body(True))
    pl.when(j > 0)(lambda: body(False))


def _inproj(x2, norm_w, w_in_bf16, tab_a, tab_b, seq, tm):
    tokens, d_model = x2.shape
    in_cols = w_in_bf16.shape[1]
    n_groups = in_cols // GROUP_WIDTH
    tiles_per_seq = seq // tm

    def table(i, j):
        rotary = (j == 1) | (j == 3) | (j == 4)
        kind = jnp.where(j == 0, 0, jnp.where(rotary, 1, 2))
        return (kind, i % tiles_per_seq, 0)

    return pl.pallas_call(
        _inproj_kernel,
        grid=(tokens // tm, n_groups),
        in_specs=[
            pl.BlockSpec((tm, d_model), lambda i, j: (i, 0)),
            pl.BlockSpec((1, d_model), lambda i, j: (0, 0)),
            pl.BlockSpec((d_model, GROUP_WIDTH), lambda i, j: (0, j)),
            pl.BlockSpec((None, tm, HEAD_DIM), table),
            pl.BlockSpec((None, tm, HEAD_DIM), table),
        ],
        out_specs=pl.BlockSpec((tm, GROUP_WIDTH), lambda i, j: (i, j)),
        out_shape=jax.ShapeDtypeStruct((tokens, in_cols), BF16),
        scratch_shapes=[pltpu.VMEM((tm, d_model), BF16)],
        compiler_params=pltpu.CompilerParams(
            dimension_semantics=("arbitrary", "arbitrary"),
            vmem_limit_bytes=V7X_VMEM_LIMIT_BYTES),
        name="inproj",
    )(x2, norm_w, w_in_bf16, tab_a, tab_b)


def _mixer_kernel(logg_ref, q_ref, k_ref, v_ref, qr_ref, kr_ref, vr_ref, g_ref, gw_ref, gb_ref, *rest):
    n_cast = (len(rest) - 8) // 2
    cast_in, a_ref, r_ref = rest[:n_cast], rest[n_cast], rest[n_cast + 1]
    cast_out = rest[n_cast + 2:2 * n_cast + 2]
    kaug_ref, vt_ref, kmean_ref, qt_ref, s_ref, p_ref = rest[2 * n_cast + 2:]
    seq = q_ref.shape[0]
    nb = seq // MOBA_BLOCK

    for src, dst in zip(cast_in, cast_out):
        if len(dst.shape) == 2:
            dst[...] = src[...].astype(BF16)
        else:
            width = dst.shape[2]
            for c in range(dst.shape[0]):
                dst[c] = src[:, c * width:(c + 1) * width].astype(BF16)

    lane = lax.broadcasted_iota(jnp.int32, (MOBA_BLOCK, HEAD_DIM), 1)
    for jb in range(nb):
        rows = slice(jb * MOBA_BLOCK, (jb + 1) * MOBA_BLOCK)
        kb = k_ref[rows, :]
        kaug_ref[rows, :HEAD_DIM] = kb
        kaug_ref[rows, HEAD_DIM:] = jnp.where(lane == jb, 1.0, 0.0).astype(BF16)
        vt_ref[:, rows] = v_ref[rows, :].astype(F32).T.astype(BF16)
        kmean_ref[jb:jb + 1, :] = jnp.mean(kb.astype(F32), axis=0, keepdims=True)
    for slot in range(2):
        qt_ref[slot, HEAD_DIM:, :] = jnp.zeros((HEAD_DIM, MOBA_BLOCK), BF16)
    kmean = kmean_ref[...].astype(BF16)

    blk_idx = lax.broadcasted_iota(jnp.int32, (nb, MOBA_BLOCK), 0)
    key_pos = lax.broadcasted_iota(jnp.int32, (MOBA_BLOCK, MOBA_BLOCK), 0)
    qry_pos = lax.broadcasted_iota(jnp.int32, (MOBA_BLOCK, MOBA_BLOCK), 1)
    causal = key_pos <= qry_pos

    col_max = {}

    def score_stage(i):
        slot = i % 2
        qt = q_ref[i * MOBA_BLOCK:(i + 1) * MOBA_BLOCK, :].astype(F32).T.astype(BF16)
        past = blk_idx < i
        gate = jnp.where(past, _dot(kmean, qt), -jnp.inf)
        beaten_by = jnp.zeros((nb, MOBA_BLOCK), jnp.int32)
        for jp in range(i):
            row = gate[jp:jp + 1, :]
            beats = (row > gate) | ((row == gate) & (jp < blk_idx))
            beaten_by = beaten_by + beats.astype(jnp.int32)
        visible = (past & (beaten_by < MOBA_TOPK)) | (blk_idx == i)
        qt_ref[slot, :HEAD_DIM, :] = qt
        qt_ref[slot, HEAD_DIM:HEAD_DIM + nb, :] = jnp.where(visible, 0.0, MASKED_SCORE).astype(BF16)
        qt_aug = qt_ref[slot]
        yield
        m = None
        for t in range(i + 1):
            k_rows = slice(t * MOBA_BLOCK, (t + 1) * MOBA_BLOCK)
            s = _dot(kaug_ref[k_rows, :], qt_aug)
            if t == i:
                s = jnp.where(causal, s, MASKED_SCORE)
            s_ref[slot, k_rows, :] = s
            tile_max = jnp.max(s, axis=0, keepdims=True)
            m = tile_max if m is None else jnp.maximum(m, tile_max)
            yield
        col_max[i] = m

    def value_stage(i):
        slot = i % 2
        m = col_max.pop(i)
        l = None
        for t in range(i + 1):
            k_rows = slice(t * MOBA_BLOCK, (t + 1) * MOBA_BLOCK)
            p = jnp.exp2(s_ref[slot, k_rows, :] - m)
            tile_sum = jnp.sum(p, axis=0, keepdims=True)
            l = tile_sum if l is None else l + tile_sum
            p_ref[slot, k_rows, :] = p.astype(BF16)
            yield
        n_keys = (i + 1) * MOBA_BLOCK
        out_t = _dot(vt_ref[:, :n_keys], p_ref[slot, :n_keys, :]) / l
        a_ref[i * MOBA_BLOCK:(i + 1) * MOBA_BLOCK, :] = out_t.T.astype(BF16)
        yield

    def retention_stage():
        log_g = logg_ref[pl.program_id(1)]
        row = lax.broadcasted_iota(jnp.int32, (RET_CHUNK, RET_CHUNK), 0)
        colm = lax.broadcasted_iota(jnp.int32, (RET_CHUNK, RET_CHUNK), 1)
        rel = (row - colm).astype(F32)
        decay = jnp.where(rel >= 0, jnp.exp(log_g * jnp.maximum(rel, 0.0)), 0.0) * ATTN_SCALE
        pos = lax.broadcasted_iota(jnp.int32, (RET_CHUNK, HEAD_DIM), 0).astype(F32)
        xi = jnp.exp(log_g * (pos + 1.0))
        zeta = jnp.exp(log_g * (RET_CHUNK - 1.0 - pos)) * ATTN_SCALE
        g_chunk = jnp.exp(log_g * RET_CHUNK)
        gn_w = gw_ref[...]
        gn_b = gb_ref[...]
        state = jnp.zeros((HEAD_DIM, HEAD_DIM), F32)
        for c in range(seq // RET_CHUNK):
            rows = slice(c * RET_CHUNK, (c + 1) * RET_CHUNK)
            qc = qr_ref[rows, :]
            kc = kr_ref[rows, :]
            vc = vr_ref[rows, :]
            inner = _dot_nt(qc, kc) * decay
            y = _dot(inner.astype(BF16), vc)
            if c > 0:
                y = y + _dot(qc, state.astype(BF16)) * xi
            kz_t = (kc.astype(F32) * zeta).T.astype(BF16)
            state = state * g_chunk + _dot(kz_t, vc)
            mu = jnp.mean(y, axis=-1, keepdims=True)
            yc = y - mu
            var = jnp.mean(yc * yc, axis=-1, keepdims=True)
            r = yc * lax.rsqrt(var + GN_EPS) * gn_w + gn_b
            g = g_ref[rows, :].astype(F32)
            r_ref[rows, :] = (g * jax.nn.sigmoid(g) * r).astype(BF16)
            yield

    retention = retention_stage()
    for _ in score_stage(0):
        pass
    for i in range(nb):
        stages = [value_stage(i)] + ([score_stage(i + 1)] if i + 1 < nb else [])
        first_pass = True
        while stages:
            for stage in list(stages):
                if next(stage, "done") == "done":
                    stages.remove(stage)
            if first_pass:
                next(retention, None)
                first_pass = False
    for _ in retention:
        pass


def _mixer(proj3, log_g, gn_w, gn_b, weights_f32):
    batch, seq, _ = proj3.shape
    nb = seq // MOBA_BLOCK
    assert nb <= HEAD_DIM, "block-selection rows must fit in the spare contraction rows"
    n_steps = batch * N_HEADS

    def col(group):
        return pl.BlockSpec((None, seq, HEAD_DIM), lambda b, h, lg: (b, 0, group * N_HEADS + h))

    head_vec = pl.BlockSpec((1, HEAD_DIM), lambda b, h, lg: (0, h))
    head_out = pl.BlockSpec((None, seq, HEAD_DIM), lambda b, h, lg: (b, 0, h))

    cast_in, cast_out, cast_shapes = [], [], []
    for w, col_chunk in weights_f32:
        n_rows, n_cols = w.shape
        rows = n_rows // n_steps
        assert rows * n_steps == n_rows and rows % BF16_SUBLANES == 0
        cast_in.append(pl.BlockSpec((rows, n_cols), lambda b, h, lg: (b * N_HEADS + h, 0)))
        if col_chunk is None:
            cast_out.append(cast_in[-1])
            cast_shapes.append(jax.ShapeDtypeStruct(w.shape, BF16))
        else:
            n_chunks = n_cols // col_chunk
            cast_out.append(pl.BlockSpec((n_chunks, rows, col_chunk),
                                         lambda b, h, lg: (0, b * N_HEADS + h, 0)))
            cast_shapes.append(jax.ShapeDtypeStruct((n_chunks, n_rows, col_chunk), BF16))

    head_shape = jax.ShapeDtypeStruct((batch, seq, GROUP_WIDTH), BF16)
    outs = pl.pallas_call(
        _mixer_kernel,
        grid_spec=pltpu.PrefetchScalarGridSpec(
            num_scalar_prefetch=1,
            grid=(batch, N_HEADS),
            in_specs=[col(0), col(1), col(2), col(3), col(4), col(5), col(6), head_vec, head_vec]
            + cast_in,
            out_specs=[head_out, head_out] + cast_out,
            scratch_shapes=[
                pltpu.VMEM((seq, 2 * HEAD_DIM), BF16),
                pltpu.VMEM((HEAD_DIM, seq), BF16),
                pltpu.VMEM((nb, HEAD_DIM), F32),
                pltpu.VMEM((2, 2 * HEAD_DIM, MOBA_BLOCK), BF16),
                pltpu.VMEM((2, seq, MOBA_BLOCK), F32),
                pltpu.VMEM((2, seq, MOBA_BLOCK), BF16),
            ],
        ),
        out_shape=[head_shape, head_shape] + cast_shapes,
        compiler_params=pltpu.CompilerParams(
            dimension_semantics=("arbitrary", "arbitrary"),
            vmem_limit_bytes=V7X_VMEM_LIMIT_BYTES),
        name="mixer",
    )(log_g, *([proj3] * 7), gn_w, gn_b, *[w for w, _ in weights_f32])
    return outs[0], outs[1], outs[2:]


def _outproj_kernel(a_ref, r_ref, w_ref, x_ref, post_ref, pre_ref, x1_ref, hn_ref):
    for rows in _sub_tiles(x_ref.shape[0]):
        mix = (_dot(a_ref[rows, :], w_ref[:GROUP_WIDTH, :])
               + _dot(r_ref[rows, :], w_ref[GROUP_WIDTH:, :]))
        x1 = x_ref[rows, :] + mix * _rms_scale(mix) * post_ref[...]
        x1_ref[rows, :] = x1
        hn_ref[rows, :] = (x1 * _rms_scale(x1) * pre_ref[...]).astype(BF16)


def _outproj(a2, r2, w_out_bf16, x2, norm_post, norm_pre, tm):
    tokens, d_model = x2.shape
    row_tile = lambda width: pl.BlockSpec((tm, width), lambda i: (i, 0))
    whole = lambda shape: pl.BlockSpec(shape, lambda i: (0, 0))
    return pl.pallas_call(
        _outproj_kernel,
        grid=(tokens // tm,),
        in_specs=[row_tile(GROUP_WIDTH), row_tile(GROUP_WIDTH), whole(w_out_bf16.shape),
                  row_tile(d_model), whole((1, d_model)), whole((1, d_model))],
        out_specs=[row_tile(d_model), row_tile(d_model)],
        out_shape=[jax.ShapeDtypeStruct((tokens, d_model), F32),
                   jax.ShapeDtypeStruct((tokens, d_model), BF16)],
        compiler_params=pltpu.CompilerParams(
            dimension_semantics=("arbitrary",),
            vmem_limit_bytes=V7X_VMEM_LIMIT_BYTES),
        name="outproj",
    )(a2, r2, w_out_bf16, x2, norm_post, norm_pre)


def _ffn_kernel(hn_ref, wg_ref, wv_ref, cw_ref, cb_ref, wd_ref, x1c_ref, post_ref,
                o_ref, x1_ref, carry_ref, *, tiles_per_seq):
    i = pl.program_id(0)
    j = pl.program_id(1)
    last = pl.num_programs(1) - 1
    tf = wg_ref.shape[1]
    n_x1, _, x1_cols = x1_ref.shape

    @pl.when(i % tiles_per_seq == 0)
    def _():
        carry_ref[j] = jnp.zeros((8, tf), F32)

    @pl.when(j < n_x1)
    def _():
        x1_ref[j] = x1c_ref[...]

    def body(first_chunk, last_chunk):
        cw = cw_ref[...]
        cb = cb_ref[...]
        history = carry_ref[j]
        row = lax.broadcasted_iota(jnp.int32, (SUB_ROWS, tf), 0)
        tiles = _sub_tiles(hn_ref.shape[0])
        ug, uv, hidden = {}, {}, {}

        def up(s):
            hn = hn_ref[tiles[s], :]
            ug[s] = _dot(hn, wg_ref[...])
            uv[s] = _dot(hn, wv_ref[...])

        def conv(s):
            if s == 0:
                prev2, prev1 = history[6:7, :], history[7:8, :]
            else:
                prev2, prev1 = ug[s - 1][SUB_ROWS - 2:SUB_ROWS - 1, :], ug[s - 1][SUB_ROWS - 1:, :]
            g = ug[s]
            back1 = jnp.where(row == 0, prev1, pltpu.roll(g, 1, axis=0))
            back2 = jnp.where(row == 0, prev2,
                              jnp.where(row == 1, prev1, pltpu.roll(g, 2, axis=0)))
            c = cw[0:1, :] * back2 + cw[1:2, :] * back1 + cw[2:3, :] * g + cb
            hidden[s] = (c * jax.nn.sigmoid(c) * uv.pop(s)).astype(BF16)

        def down(s):
            rows = tiles[s]
            part = _dot(hidden.pop(s), wd_ref[...])
            ff = part if first_chunk else o_ref[rows, :] + part
            if last_chunk:
                normed = ff * _rms_scale(ff) * post_ref[...]
                for c in range(n_x1):
                    cols = slice(c * x1_cols, (c + 1) * x1_cols)
                    o_ref[rows, cols] = x1_ref[c, rows, :] + normed[:, cols]
            else:
                o_ref[rows, :] = ff

        n_sub = len(tiles)
        up(0)
        for s in range(n_sub):
            if s + 1 < n_sub:
                up(s + 1)
            if s >= 1:
                down(s - 1)
            conv(s)
        down(n_sub - 1)
        carry_ref[j] = ug[n_sub - 1][SUB_ROWS - 8:, :]

    pl.when(j == 0)(lambda: body(True, False))
    pl.when((j > 0) & (j < last))(lambda: body(False, False))
    pl.when(j == last)(lambda: body(False, True))


def _ffn(hn2, w_up_bf16, conv_w, conv_b, w_down_bf16, x1, norm_post, seq, tm, tf):
    tokens, d_model = x1.shape
    d_ff = w_down_bf16.shape[0]
    n_chunks = d_ff // tf
    n_x1 = d_model // X1_SLAB_COLS
    assert n_chunks >= 2 and n_chunks >= n_x1
    row_tile = pl.BlockSpec((tm, d_model), lambda i, j: (i, 0))
    return pl.pallas_call(
        functools.partial(_ffn_kernel, tiles_per_seq=seq // tm),
        grid=(tokens // tm, n_chunks),
        in_specs=[
            row_tile,
            pl.BlockSpec((None, d_model, tf), lambda i, j: (j, 0, 0)),
            pl.BlockSpec((None, d_model, tf), lambda i, j: (n_chunks + j, 0, 0)),
            pl.BlockSpec((CONV_WIDTH, tf), lambda i, j: (0, j)),
            pl.BlockSpec((1, tf), lambda i, j: (0, j)),
            pl.BlockSpec((tf, d_model), lambda i, j: (j, 0)),
            pl.BlockSpec((tm, X1_SLAB_COLS), lambda i, j: (i, jnp.minimum(j, n_x1 - 1))),
            pl.BlockSpec((1, d_model), lambda i, j: (0, 0)),
        ],
        out_specs=row_tile,
        out_shape=jax.ShapeDtypeStruct((tokens, d_model), F32),
        scratch_shapes=[pltpu.VMEM((n_x1, tm, X1_SLAB_COLS), F32),
                        pltpu.VMEM((n_chunks, 8, tf), F32)],
        compiler_params=pltpu.CompilerParams(
            dimension_semantics=("arbitrary", "arbitrary"),
            vmem_limit_bytes=V7X_VMEM_LIMIT_BYTES),
        name="ffn",
    )(hn2, w_up_bf16, w_up_bf16, conv_w, conv_b, w_down_bf16, x1, norm_post)


def _rope_tables(seq):
    pos = np.arange(seq, dtype=np.float32)
    inv = (1.0 / (ROPE_THETA ** (np.arange(0, HEAD_DIM, 2, dtype=np.float32) / HEAD_DIM))).astype(np.float32)
    ang = pos[:, None] * inv[None, :]
    cos, sin = np.cos(ang), np.sin(ang)
    cos_full = np.concatenate([cos, cos], axis=-1)
    sin_signed = np.concatenate([-sin, sin], axis=-1)
    q_scale = np.float32(MOBA_Q_SCALE)
    tab_a = np.stack([cos_full * q_scale, cos_full, np.ones_like(cos_full)])
    tab_b = np.stack([sin_signed * q_scale, sin_signed, np.zeros_like(sin_signed)])
    return jnp.asarray(tab_a, F32), jnp.asarray(tab_b, F32)


def _retention_log_decay():
    heads = np.arange(N_HEADS, dtype=np.float32)
    return jnp.asarray(np.log(np.float32(1.0) - np.float32(2.0) ** (-5.0 - heads)), F32)


def _tile(n, preferred):
    for t in preferred:
        if n % t == 0:
            return t
    raise ValueError(f"no supported tile for extent {n}")


def kernel(x, norm_mix_pre, w_in, ret_gn_w, ret_gn_b, w_out, norm_mix_post, norm_ffn_pre, w_up, conv_w, conv_b, w_down, norm_ffn_post):
    batch, seq, d_model = x.shape
    depth = w_in.shape[0]
    assert seq % MOBA_BLOCK == 0 and seq % RET_CHUNK == 0
    assert w_in.shape[2] == 7 * GROUP_WIDTH and w_out.shape[1] == 2 * GROUP_WIDTH
    d_ff = w_down.shape[1]
    tokens = batch * seq
    tm_proj = _tile(seq, (1024, 512, 256))
    tm_out = _tile(seq, (512, 256))
    tm_ffn = _tile(seq, (1024, 512, 256))
    tf = _tile(d_ff, (512, 256, 128))

    tab_a, tab_b = _rope_tables(seq)
    log_g = _retention_log_decay()

    x2 = x.reshape(tokens, d_model)
    for layer in range(depth):
        proj = _inproj(x2, norm_mix_pre[layer][None, :], w_in[layer].astype(BF16),
                       tab_a, tab_b, seq, tm_proj)
        proj3 = proj.reshape(batch, seq, -1)
        a, r, (w_out_bf16, w_up_bf16, w_down_bf16) = _mixer(
            proj3, log_g, ret_gn_w[layer][None, :], ret_gn_b[layer][None, :],
            ((w_out[layer], None), (w_up[layer], tf), (w_down[layer], None)))
        x1, hn2 = _outproj(a.reshape(tokens, GROUP_WIDTH), r.reshape(tokens, GROUP_WIDTH),
                           w_out_bf16, x2,
                           norm_mix_post[layer][None, :], norm_ffn_pre[layer][None, :], tm_out)
        x2 = _ffn(hn2, w_up_bf16, conv_w[layer], conv_b[layer][None, :],
                  w_down_bf16, x1, norm_ffn_post[layer][None, :], seq, tm_ffn, tf)
    return x2.reshape(batch, seq, d_model)
```

```python
import functools

import jax
import jax.numpy as jnp
import numpy as np
from jax import lax
from jax.experimental import pallas as pl
from jax.experimental.pallas import tpu as pltpu

HEAD_DIM = 128
N_HEADS = 8
GROUP_WIDTH = N_HEADS * HEAD_DIM
MOBA_BLOCK = 256
MOBA_TOPK = 3
RET_CHUNK = 256
ROPE_THETA = 10000.0
CONV_WIDTH = 3
NORM_EPS = 1e-6
GN_EPS = 1e-5
ATTN_SCALE = HEAD_DIM ** -0.5
MASKED_SCORE = -1e30
MOBA_Q_SCALE = ATTN_SCALE * 1.4426950408889634

V7X_VMEM_LIMIT_BYTES = 56 * 1024 * 1024
SUB_ROWS = 256
X1_SLAB_COLS = 256
F32_SUBLANES = 8
BF16_SUBLANES = 16

F32 = jnp.float32
BF16 = jnp.bfloat16


def _dot(a, b):
    return jnp.dot(a, b, preferred_element_type=F32)


def _dot_nt(a, b):
    return lax.dot_general(a, b, (((1,), (1,)), ((), ())), preferred_element_type=F32)


def _rms_scale(v):
    return lax.rsqrt(jnp.mean(v * v, axis=-1, keepdims=True) + NORM_EPS)


def _sub_tiles(rows):
    return [slice(r, r + SUB_ROWS) for r in range(0, rows, SUB_ROWS)]


ROTARY_SCALED, ROTARY, PLAIN = 0, 1, 2
INPROJ_GROUP_KIND = (ROTARY_SCALED, ROTARY, PLAIN, ROTARY, ROTARY, PLAIN, PLAIN)


def _inproj_kernel(x_ref, nw_ref, w_ref, a_ref, b_ref, o_ref):
    x = x_ref[...]
    hn = (x * _rms_scale(x) * nw_ref[...]).astype(BF16)
    acc = {}

    def project(g):
        acc[g] = _dot(hn, w_ref[:, g * GROUP_WIDTH:(g + 1) * GROUP_WIDTH])

    def finish(g):
        kind = INPROJ_GROUP_KIND[g]
        t_all = acc.pop(g)
        if kind == PLAIN:
            o_ref[:, g * GROUP_WIDTH:(g + 1) * GROUP_WIDTH] = t_all.astype(BF16)
            return
        a = a_ref[kind]
        b = b_ref[kind]
        for h in range(N_HEADS):
            t = t_all[:, h * HEAD_DIM:(h + 1) * HEAD_DIM]
            rot = pltpu.roll(t, HEAD_DIM // 2, axis=1)
            cols = slice(g * GROUP_WIDTH + h * HEAD_DIM, g * GROUP_WIDTH + (h + 1) * HEAD_DIM)
            o_ref[:, cols] = (t * a + rot * b).astype(BF16)

    n_groups = len(INPROJ_GROUP_KIND)
    project(0)
    for g in range(n_groups):
        if g + 1 < n_groups:
            project(g + 1)
        finish(g)


def _inproj(x2, norm_w, w_in_bf16, tab_a, tab_b, seq, tm):
    tokens, d_model = x2.shape
    in_cols = w_in_bf16.shape[1]
    assert in_cols == len(INPROJ_GROUP_KIND) * GROUP_WIDTH
    tiles_per_seq = seq // tm
    table = pl.BlockSpec((tab_a.shape[0], tm, HEAD_DIM), lambda i: (0, i % tiles_per_seq, 0))
    return pl.pallas_call(
        _inproj_kernel,
        grid=(tokens // tm,),
        in_specs=[
            pl.BlockSpec((tm, d_model), lambda i: (i, 0)),
            pl.BlockSpec((1, d_model), lambda i: (0, 0)),
            pl.BlockSpec((d_model, in_cols), lambda i: (0, 0), pipeline_mode=pl.Buffered(1)),
            table,
            table,
        ],
        out_specs=pl.BlockSpec((tm, in_cols), lambda i: (i, 0)),
        out_shape=jax.ShapeDtypeStruct((tokens, in_cols), BF16),
        compiler_params=pltpu.CompilerParams(
            dimension_semantics=("arbitrary",),
            vmem_limit_bytes=V7X_VMEM_LIMIT_BYTES),
        name="inproj",
    )(x2, norm_w, w_in_bf16, tab_a, tab_b)


def _mixer_kernel(logg_ref, q_ref, k_ref, v_ref, qr_ref, kr_ref, vr_ref, g_ref, gw_ref, gb_ref, *rest):
    n_cast = (len(rest) - 8) // 2
    cast_in, a_ref, r_ref = rest[:n_cast], rest[n_cast], rest[n_cast + 1]
    cast_out = rest[n_cast + 2:2 * n_cast + 2]
    kaug_ref, vt_ref, kmean_ref, qt_ref, s_ref, p_ref = rest[2 * n_cast + 2:]
    seq = q_ref.shape[0]
    nb = seq // MOBA_BLOCK

    for src, dst in zip(cast_in, cast_out):
        if len(dst.shape) == 2:
            dst[...] = src[...].astype(BF16)
        else:
            width = dst.shape[2]
            for c in range(dst.shape[0]):
                dst[c] = src[:, c * width:(c + 1) * width].astype(BF16)

    lane = lax.broadcasted_iota(jnp.int32, (MOBA_BLOCK, HEAD_DIM), 1)
    for jb in range(nb):
        rows = slice(jb * MOBA_BLOCK, (jb + 1) * MOBA_BLOCK)
        kb = k_ref[rows, :]
        kaug_ref[rows, :HEAD_DIM] = kb
        kaug_ref[rows, HEAD_DIM:] = jnp.where(lane == jb, 1.0, 0.0).astype(BF16)
        vt_ref[:, rows] = v_ref[rows, :].astype(F32).T.astype(BF16)
        kmean_ref[jb:jb + 1, :] = jnp.mean(kb.astype(F32), axis=0, keepdims=True)
    for slot in range(2):
        qt_ref[slot, HEAD_DIM:, :] = jnp.zeros((HEAD_DIM, MOBA_BLOCK), BF16)
    kmean = kmean_ref[...].astype(BF16)

    blk_idx = lax.broadcasted_iota(jnp.int32, (nb, MOBA_BLOCK), 0)
    key_pos = lax.broadcasted_iota(jnp.int32, (MOBA_BLOCK, MOBA_BLOCK), 0)
    qry_pos = lax.broadcasted_iota(jnp.int32, (MOBA_BLOCK, MOBA_BLOCK), 1)
    causal = key_pos <= qry_pos

    col_max = {}

    def score_stage(i):
        slot = i % 2
        qt = q_ref[i * MOBA_BLOCK:(i + 1) * MOBA_BLOCK, :].astype(F32).T.astype(BF16)
        past = blk_idx < i
        gate = jnp.where(past, _dot(kmean, qt), -jnp.inf)
        beaten_by = jnp.zeros((nb, MOBA_BLOCK), jnp.int32)
        for jp in range(i):
            row = gate[jp:jp + 1, :]
            beats = (row > gate) | ((row == gate) & (jp < blk_idx))
            beaten_by = beaten_by + beats.astype(jnp.int32)
        visible = (past & (beaten_by < MOBA_TOPK)) | (blk_idx == i)
        qt_ref[slot, :HEAD_DIM, :] = qt
        qt_ref[slot, HEAD_DIM:HEAD_DIM + nb, :] = jnp.where(visible, 0.0, MASKED_SCORE).astype(BF16)
        qt_aug = qt_ref[slot]
        yield
        m = None
        for t in range(i + 1):
            k_rows = slice(t * MOBA_BLOCK, (t + 1) * MOBA_BLOCK)
            s = _dot(kaug_ref[k_rows, :], qt_aug)
            if t == i:
                s = jnp.where(causal, s, MASKED_SCORE)
            s_ref[slot, k_rows, :] = s
            tile_max = jnp.max(s, axis=0, keepdims=True)
            m = tile_max if m is None else jnp.maximum(m, tile_max)
            yield
        col_max[i] = m

    def value_stage(i):
        slot = i % 2
        m = col_max.pop(i)
        l = None
        for t in range(i + 1):
            k_rows = slice(t * MOBA_BLOCK, (t + 1) * MOBA_BLOCK)
            p = jnp.exp2(s_ref[slot, k_rows, :] - m)
            tile_sum = jnp.sum(p, axis=0, keepdims=True)
            l = tile_sum if l is None else l + tile_sum
            p_ref[slot, k_rows, :] = p.astype(BF16)
            yield
        n_keys = (i + 1) * MOBA_BLOCK
        out_t = _dot(vt_ref[:, :n_keys], p_ref[slot, :n_keys, :]) / l
        a_ref[i * MOBA_BLOCK:(i + 1) * MOBA_BLOCK, :] = out_t.T.astype(BF16)
        yield

    def retention_stage():
        log_g = logg_ref[pl.program_id(1)]
        row = lax.broadcasted_iota(jnp.int32, (RET_CHUNK, RET_CHUNK), 0)
        colm = lax.broadcasted_iota(jnp.int32, (RET_CHUNK, RET_CHUNK), 1)
        rel = (row - colm).astype(F32)
        decay = jnp.where(rel >= 0, jnp.exp(log_g * jnp.maximum(rel, 0.0)), 0.0) * ATTN_SCALE
        pos = lax.broadcasted_iota(jnp.int32, (RET_CHUNK, HEAD_DIM), 0).astype(F32)
        xi = jnp.exp(log_g * (pos + 1.0))
        zeta = jnp.exp(log_g * (RET_CHUNK - 1.0 - pos)) * ATTN_SCALE
        g_chunk = jnp.exp(log_g * RET_CHUNK)
        gn_w = gw_ref[...]
        gn_b = gb_ref[...]
        state = jnp.zeros((HEAD_DIM, HEAD_DIM), F32)
        for c in range(seq // RET_CHUNK):
            rows = slice(c * RET_CHUNK, (c + 1) * RET_CHUNK)
            qc = qr_ref[rows, :]
            kc = kr_ref[rows, :]
            vc = vr_ref[rows, :]
            inner = _dot_nt(qc, kc) * decay
            y = _dot(inner.astype(BF16), vc)
            if c > 0:
                y = y + _dot(qc, state.astype(BF16)) * xi
            kz_t = (kc.astype(F32) * zeta).T.astype(BF16)
            state = state * g_chunk + _dot(kz_t, vc)
            mu = jnp.mean(y, axis=-1, keepdims=True)
            yc = y - mu
            var = jnp.mean(yc * yc, axis=-1, keepdims=True)
            r = yc * lax.rsqrt(var + GN_EPS) * gn_w + gn_b
            g = g_ref[rows, :].astype(F32)
            r_ref[rows, :] = (g * jax.nn.sigmoid(g) * r).astype(BF16)
            yield

    retention = retention_stage()
    for _ in score_stage(0):
        pass
    for i in range(nb):
        stages = [value_stage(i)] + ([score_stage(i + 1)] if i + 1 < nb else [])
        first_pass = True
        while stages:
            for stage in list(stages):
                if next(stage, "done") == "done":
                    stages.remove(stage)
            if first_pass:
                next(retention, None)
                first_pass = False
    for _ in retention:
        pass


def _mixer(proj3, log_g, gn_w, gn_b, weights_f32):
    batch, seq, _ = proj3.shape
    nb = seq // MOBA_BLOCK
    assert nb <= HEAD_DIM, "block-selection rows must fit in the spare contraction rows"
    n_steps = batch * N_HEADS

    def col(group):
        return pl.BlockSpec((None, seq, HEAD_DIM), lambda b, h, lg: (b, 0, group * N_HEADS + h))

    head_vec = pl.BlockSpec((1, HEAD_DIM), lambda b, h, lg: (0, h))
    head_out = pl.BlockSpec((None, seq, HEAD_DIM), lambda b, h, lg: (b, 0, h))

    cast_in, cast_out, cast_shapes = [], [], []
    for w, col_chunk in weights_f32:
        n_rows, n_cols = w.shape
        rows = n_rows // n_steps
        assert rows * n_steps == n_rows and rows % BF16_SUBLANES == 0
        cast_in.append(pl.BlockSpec((rows, n_cols), lambda b, h, lg: (b * N_HEADS + h, 0)))
        if col_chunk is None:
            cast_out.append(cast_in[-1])
            cast_shapes.append(jax.ShapeDtypeStruct(w.shape, BF16))
        else:
            n_chunks = n_cols // col_chunk
            cast_out.append(pl.BlockSpec((n_chunks, rows, col_chunk),
                                         lambda b, h, lg: (0, b * N_HEADS + h, 0)))
            cast_shapes.append(jax.ShapeDtypeStruct((n_chunks, n_rows, col_chunk), BF16))

    head_shape = jax.ShapeDtypeStruct((batch, seq, GROUP_WIDTH), BF16)
    outs = pl.pallas_call(
        _mixer_kernel,
        grid_spec=pltpu.PrefetchScalarGridSpec(
            num_scalar_prefetch=1,
            grid=(batch, N_HEADS),
            in_specs=[col(0), col(1), col(2), col(3), col(4), col(5), col(6), head_vec, head_vec]
            + cast_in,
            out_specs=[head_out, head_out] + cast_out,
            scratch_shapes=[
                pltpu.VMEM((seq, 2 * HEAD_DIM), BF16),
                pltpu.VMEM((HEAD_DIM, seq), BF16),
                pltpu.VMEM((nb, HEAD_DIM), F32),
                pltpu.VMEM((2, 2 * HEAD_DIM, MOBA_BLOCK), BF16),
                pltpu.VMEM((2, seq, MOBA_BLOCK), F32),
                pltpu.VMEM((2, seq, MOBA_BLOCK), BF16),
            ],
        ),
        out_shape=[head_shape, head_shape] + cast_shapes,
        compiler_params=pltpu.CompilerParams(
            dimension_semantics=("arbitrary", "arbitrary"),
            vmem_limit_bytes=V7X_VMEM_LIMIT_BYTES),
        name="mixer",
    )(log_g, *([proj3] * 7), gn_w, gn_b, *[w for w, _ in weights_f32])
    return outs[0], outs[1], outs[2:]


def _outproj_kernel(a_ref, r_ref, w_ref, x_ref, post_ref, pre_ref, x1_ref, hn_ref):
    for rows in _sub_tiles(x_ref.shape[0]):
        mix = (_dot(a_ref[rows, :], w_ref[:GROUP_WIDTH, :])
               + _dot(r_ref[rows, :], w_ref[GROUP_WIDTH:, :]))
        x1 = x_ref[rows, :] + mix * _rms_scale(mix) * post_ref[...]
        x1_ref[rows, :] = x1
        hn_ref[rows, :] = (x1 * _rms_scale(x1) * pre_ref[...]).astype(BF16)


def _outproj(a2, r2, w_out_bf16, x2, norm_post, norm_pre, tm):
    tokens, d_model = x2.shape
    row_tile = lambda width: pl.BlockSpec((tm, width), lambda i: (i, 0))
    whole = lambda shape: pl.BlockSpec(shape, lambda i: (0, 0))
    return pl.pallas_call(
        _outproj_kernel,
        grid=(tokens // tm,),
        in_specs=[row_tile(GROUP_WIDTH), row_tile(GROUP_WIDTH), whole(w_out_bf16.shape),
                  row_tile(d_model), whole((1, d_model)), whole((1, d_model))],
        out_specs=[row_tile(d_model), row_tile(d_model)],
        out_shape=[jax.ShapeDtypeStruct((tokens, d_model), F32),
                   jax.ShapeDtypeStruct((tokens, d_model), BF16)],
        compiler_params=pltpu.CompilerParams(
            dimension_semantics=("arbitrary",),
            vmem_limit_bytes=V7X_VMEM_LIMIT_BYTES),
        name="outproj",
    )(a2, r2, w_out_bf16, x2, norm_post, norm_pre)


def _ffn_kernel(hn_ref, wg_ref, wv_ref, cw_ref, cb_ref, wd_ref, x1c_ref, post_ref,
                o_ref, x1_ref, carry_ref, *, tiles_per_seq):
    i = pl.program_id(0)
    j = pl.program_id(1)
    last = pl.num_programs(1) - 1
    tf = wg_ref.shape[1]
    n_x1, _, x1_cols = x1_ref.shape

    @pl.when(i % tiles_per_seq == 0)
    def _():
        carry_ref[j] = jnp.zeros((F32_SUBLANES, tf), F32)

    @pl.when(j < n_x1)
    def _():
        x1_ref[j] = x1c_ref[...]

    def body(first_chunk, last_chunk):
        cw = cw_ref[...]
        cb = cb_ref[...]
        history = carry_ref[j]
        row = lax.broadcasted_iota(jnp.int32, (SUB_ROWS, tf), 0)
        tiles = _sub_tiles(hn_ref.shape[0])
        ug, uv, hidden = {}, {}, {}

        def up(s):
            hn = hn_ref[tiles[s], :]
            ug[s] = _dot(hn, wg_ref[...])
            uv[s] = _dot(hn, wv_ref[...])

        def conv(s):
            if s == 0:
                prev2 = history[F32_SUBLANES - 2:F32_SUBLANES - 1, :]
                prev1 = history[F32_SUBLANES - 1:, :]
            else:
                prev2, prev1 = ug[s - 1][SUB_ROWS - 2:SUB_ROWS - 1, :], ug[s - 1][SUB_ROWS - 1:, :]
            g = ug[s]
            back1 = jnp.where(row == 0, prev1, pltpu.roll(g, 1, axis=0))
            back2 = jnp.where(row == 0, prev2,
                              jnp.where(row == 1, prev1, pltpu.roll(g, 2, axis=0)))
            c = cw[0:1, :] * back2 + cw[1:2, :] * back1 + cw[2:3, :] * g + cb
            hidden[s] = (c * jax.nn.sigmoid(c) * uv.pop(s)).astype(BF16)

        def down(s):
            rows = tiles[s]
            part = _dot(hidden.pop(s), wd_ref[...])
            ff = part if first_chunk else o_ref[rows, :] + part
            if last_chunk:
                normed = ff * _rms_scale(ff) * post_ref[...]
                for c in range(n_x1):
                    cols = slice(c * x1_cols, (c + 1) * x1_cols)
                    o_ref[rows, cols] = x1_ref[c, rows, :] + normed[:, cols]
            else:
                o_ref[rows, :] = ff

        n_sub = len(tiles)
        up(0)
        for s in range(n_sub):
            if s + 1 < n_sub:
                up(s + 1)
            if s >= 1:
                down(s - 1)
            conv(s)
        down(n_sub - 1)
        carry_ref[j] = ug[n_sub - 1][SUB_ROWS - F32_SUBLANES:, :]

    pl.when(j == 0)(lambda: body(True, False))
    pl.when((j > 0) & (j < last))(lambda: body(False, False))
    pl.when(j == last)(lambda: body(False, True))


def _ffn(hn2, w_up_bf16, conv_w, conv_b, w_down_bf16, x1, norm_post, seq, tm, tf):
    tokens, d_model = x1.shape
    d_ff = w_down_bf16.shape[0]
    n_chunks = d_ff // tf
    n_x1 = d_model // X1_SLAB_COLS
    assert n_chunks >= 2 and n_chunks >= n_x1
    row_tile = pl.BlockSpec((tm, d_model), lambda i, j: (i, 0))
    return pl.pallas_call(
        functools.partial(_ffn_kernel, tiles_per_seq=seq // tm),
        grid=(tokens // tm, n_chunks),
        in_specs=[
            row_tile,
            pl.BlockSpec((None, d_model, tf), lambda i, j: (j, 0, 0)),
            pl.BlockSpec((None, d_model, tf), lambda i, j: (n_chunks + j, 0, 0)),
            pl.BlockSpec((CONV_WIDTH, tf), lambda i, j: (0, j)),
            pl.BlockSpec((1, tf), lambda i, j: (0, j)),
            pl.BlockSpec((tf, d_model), lambda i, j: (j, 0)),
            pl.BlockSpec((tm, X1_SLAB_COLS), lambda i, j: (i, jnp.minimum(j, n_x1 - 1))),
            pl.BlockSpec((1, d_model), lambda i, j: (0, 0)),
        ],
        out_specs=row_tile,
        out_shape=jax.ShapeDtypeStruct((tokens, d_model), F32),
        scratch_shapes=[pltpu.VMEM((n_x1, tm, X1_SLAB_COLS), F32),
                        pltpu.VMEM((n_chunks, F32_SUBLANES, tf), F32)],
        compiler_params=pltpu.CompilerParams(
            dimension_semantics=("arbitrary", "arbitrary"),
            vmem_limit_bytes=V7X_VMEM_LIMIT_BYTES),
        name="ffn",
    )(hn2, w_up_bf16, w_up_bf16, conv_w, conv_b, w_down_bf16, x1, norm_post)


def _rope_tables(seq):
    pos = np.arange(seq, dtype=np.float32)
    inv = (1.0 / (ROPE_THETA ** (np.arange(0, HEAD_DIM, 2, dtype=np.float32) / HEAD_DIM))).astype(np.float32)
    ang = pos[:, None] * inv[None, :]
    cos, sin = np.cos(ang), np.sin(ang)
    cos_full = np.concatenate([cos, cos], axis=-1)
    sin_signed = np.concatenate([-sin, sin], axis=-1)
    q_scale = np.float32(MOBA_Q_SCALE)
    tab_a = np.stack([cos_full * q_scale, cos_full])
    tab_b = np.stack([sin_signed * q_scale, sin_signed])
    return jnp.asarray(tab_a, F32), jnp.asarray(tab_b, F32)


def _retention_log_decay():
    heads = np.arange(N_HEADS, dtype=np.float32)
    return jnp.asarray(np.log(np.float32(1.0) - np.float32(2.0) ** (-5.0 - heads)), F32)


def _tile(n, preferred):
    for t in preferred:
        if n % t == 0:
            return t
    raise ValueError(f"no supported tile for extent {n}")


def kernel(x, norm_mix_pre, w_in, ret_gn_w, ret_gn_b, w_out, norm_mix_post, norm_ffn_pre, w_up, conv_w, conv_b, w_down, norm_ffn_post):
    batch, seq, d_model = x.shape
    depth = w_in.shape[0]
    assert seq % MOBA_BLOCK == 0 and seq % RET_CHUNK == 0
    assert w_in.shape[2] == 7 * GROUP_WIDTH and w_out.shape[1] == 2 * GROUP_WIDTH
    d_ff = w_down.shape[1]
    tokens = batch * seq
    tm_proj = _tile(seq, (256,))
    tm_out = _tile(seq, (512, 256))
    tm_ffn = _tile(seq, (1024, 512, 256))
    tf = _tile(d_ff, (512, 256, 128))

    tab_a, tab_b = _rope_tables(seq)
    log_g = _retention_log_decay()

    x2 = x.reshape(tokens, d_model)
    for layer in range(depth):
        proj = _inproj(x2, norm_mix_pre[layer][None, :], w_in[layer].astype(BF16),
                       tab_a, tab_b, seq, tm_proj)
        proj3 = proj.reshape(batch, seq, -1)
        a, r, (w_out_bf16, w_up_bf16, w_down_bf16) = _mixer(
            proj3, log_g, ret_gn_w[layer][None, :], ret_gn_b[layer][None, :],
            ((w_out[layer], None), (w_up[layer], tf), (w_down[layer], None)))
        x1, hn2 = _outproj(a.reshape(tokens, GROUP_WIDTH), r.reshape(tokens, GROUP_WIDTH),
                           w_out_bf16, x2,
                           norm_mix_post[layer][None, :], norm_ffn_pre[layer][None, :], tm_out)
        x2 = _ffn(hn2, w_up_bf16, conv_w[layer], conv_b[layer][None, :],
                  w_down_bf16, x1, norm_ffn_post[layer][None, :], seq, tm_ffn, tf)
    return x2.reshape(batch, seq, d_model)
```

```python
import functools

import jax
import jax.numpy as jnp
import numpy as np
from jax import lax
from jax.experimental import pallas as pl
from jax.experimental.pallas import tpu as pltpu

HEAD_DIM = 128
N_HEADS = 8
GROUP_WIDTH = N_HEADS * HEAD_DIM
MOBA_BLOCK = 256
MOBA_TOPK = 3
RET_CHUNK = 256
ROPE_THETA = 10000.0
CONV_WIDTH = 3
NORM_EPS = 1e-6
GN_EPS = 1e-5
ATTN_SCALE = HEAD_DIM ** -0.5
MASKED_SCORE = -1e30
MOBA_Q_SCALE = ATTN_SCALE * 1.4426950408889634

V7X_VMEM_LIMIT_BYTES = 56 * 1024 * 1024
SUB_ROWS = 256
X1_SLAB_COLS = 256
F32_SUBLANES = 8
BF16_SUBLANES = 16

F32 = jnp.float32
BF16 = jnp.bfloat16


def _dot(a, b):
    return jnp.dot(a, b, preferred_element_type=F32)


def _dot_nt(a, b):
    return lax.dot_general(a, b, (((1,), (1,)), ((), ())), preferred_element_type=F32)


def _rms_scale(v):
    return lax.rsqrt(jnp.mean(v * v, axis=-1, keepdims=True) + NORM_EPS)


def _sub_tiles(rows):
    return [slice(r, r + SUB_ROWS) for r in range(0, rows, SUB_ROWS)]


def _tapered_tiles(rows):
    tiles = _sub_tiles(rows)
    mid = tiles[-1].start + SUB_ROWS // 2
    return tiles[:-1] + [slice(tiles[-1].start, mid), slice(mid, tiles[-1].stop)]


ROTARY_SCALED, ROTARY, PLAIN = 0, 1, 2
INPROJ_GROUP_KIND = (ROTARY_SCALED, ROTARY, PLAIN, ROTARY, ROTARY, PLAIN, PLAIN)


def _inproj_kernel(x_ref, nw_ref, w_ref, a_ref, b_ref, o_ref):
    x = x_ref[...]
    hn = (x * _rms_scale(x) * nw_ref[...]).astype(BF16)
    acc = {}

    def project(g):
        acc[g] = _dot(hn, w_ref[:, g * GROUP_WIDTH:(g + 1) * GROUP_WIDTH])

    def finish(g):
        kind = INPROJ_GROUP_KIND[g]
        t_all = acc.pop(g)
        if kind == PLAIN:
            o_ref[:, g * GROUP_WIDTH:(g + 1) * GROUP_WIDTH] = t_all.astype(BF16)
            return
        a = a_ref[kind]
        b = b_ref[kind]
        for h in range(N_HEADS):
            t = t_all[:, h * HEAD_DIM:(h + 1) * HEAD_DIM]
            rot = pltpu.roll(t, HEAD_DIM // 2, axis=1)
            cols = slice(g * GROUP_WIDTH + h * HEAD_DIM, g * GROUP_WIDTH + (h + 1) * HEAD_DIM)
            o_ref[:, cols] = (t * a + rot * b).astype(BF16)

    n_groups = len(INPROJ_GROUP_KIND)
    project(0)
    for g in range(n_groups):
        if g + 1 < n_groups:
            project(g + 1)
        finish(g)


def _inproj(x2, norm_w, w_in_bf16, tab_a, tab_b, seq, tm):
    tokens, d_model = x2.shape
    in_cols = w_in_bf16.shape[1]
    assert in_cols == len(INPROJ_GROUP_KIND) * GROUP_WIDTH
    tiles_per_seq = seq // tm
    table = pl.BlockSpec((tab_a.shape[0], tm, HEAD_DIM), lambda i: (0, i % tiles_per_seq, 0))
    return pl.pallas_call(
        _inproj_kernel,
        grid=(tokens // tm,),
        in_specs=[
            pl.BlockSpec((tm, d_model), lambda i: (i, 0)),
            pl.BlockSpec((1, d_model), lambda i: (0, 0)),
            pl.BlockSpec((d_model, in_cols), lambda i: (0, 0), pipeline_mode=pl.Buffered(1)),
            table,
            table,
        ],
        out_specs=pl.BlockSpec((tm, in_cols), lambda i: (i, 0)),
        out_shape=jax.ShapeDtypeStruct((tokens, in_cols), BF16),
        compiler_params=pltpu.CompilerParams(
            dimension_semantics=("arbitrary",),
            vmem_limit_bytes=V7X_VMEM_LIMIT_BYTES),
        name="inproj",
    )(x2, norm_w, w_in_bf16, tab_a, tab_b)


def _mixer_kernel(logg_ref, q_ref, k_ref, v_ref, qr_ref, kr_ref, vr_ref, g_ref, gw_ref, gb_ref, *rest):
    n_cast = (len(rest) - 8) // 2
    cast_in, a_ref, r_ref = rest[:n_cast], rest[n_cast], rest[n_cast + 1]
    cast_out = rest[n_cast + 2:2 * n_cast + 2]
    kaug_ref, vt_ref, kmean_ref, qt_ref, s_ref, p_ref = rest[2 * n_cast + 2:]
    seq = q_ref.shape[0]
    nb = seq // MOBA_BLOCK

    def cast_stage():
        for src, dst in zip(cast_in, cast_out):
            if len(dst.shape) == 2:
                dst[...] = src[...].astype(BF16)
                yield
            else:
                width = dst.shape[2]
                for c in range(dst.shape[0]):
                    dst[c] = src[:, c * width:(c + 1) * width].astype(BF16)
                    yield

    lane = lax.broadcasted_iota(jnp.int32, (MOBA_BLOCK, HEAD_DIM), 1)
    kmean_ref[...] = jnp.zeros(kmean_ref.shape, F32)
    for slot in range(2):
        qt_ref[slot, HEAD_DIM:, :] = jnp.zeros((HEAD_DIM, MOBA_BLOCK), BF16)

    def prep_stage():
        for jb in range(nb):
            rows = slice(jb * MOBA_BLOCK, (jb + 1) * MOBA_BLOCK)
            kb = k_ref[rows, :]
            kaug_ref[rows, :HEAD_DIM] = kb
            kaug_ref[rows, HEAD_DIM:] = jnp.where(lane == jb, 1.0, 0.0).astype(BF16)
            vt_ref[:, rows] = v_ref[rows, :].astype(F32).T.astype(BF16)
            kmean_ref[jb:jb + 1, :] = jnp.mean(kb.astype(F32), axis=0, keepdims=True)
            yield

    blk_idx = lax.broadcasted_iota(jnp.int32, (nb, MOBA_BLOCK), 0)
    key_pos = lax.broadcasted_iota(jnp.int32, (MOBA_BLOCK, MOBA_BLOCK), 0)
    qry_pos = lax.broadcasted_iota(jnp.int32, (MOBA_BLOCK, MOBA_BLOCK), 1)
    causal = key_pos <= qry_pos

    col_max = {}

    def score_stage(i):
        slot = i % 2
        qt = q_ref[i * MOBA_BLOCK:(i + 1) * MOBA_BLOCK, :].astype(F32).T.astype(BF16)
        past = blk_idx < i
        gate = jnp.where(past, _dot(kmean_ref[...].astype(BF16), qt), -jnp.inf)
        beaten_by = jnp.zeros((nb, MOBA_BLOCK), jnp.int32)
        for jp in range(i):
            row = gate[jp:jp + 1, :]
            beats = (row > gate) | ((row == gate) & (jp < blk_idx))
            beaten_by = beaten_by + beats.astype(jnp.int32)
        visible = (past & (beaten_by < MOBA_TOPK)) | (blk_idx == i)
        qt_ref[slot, :HEAD_DIM, :] = qt
        qt_ref[slot, HEAD_DIM:HEAD_DIM + nb, :] = jnp.where(visible, 0.0, MASKED_SCORE).astype(BF16)
        qt_aug = qt_ref[slot]
        yield
        m = None
        for t in range(i + 1):
            k_rows = slice(t * MOBA_BLOCK, (t + 1) * MOBA_BLOCK)
            s = _dot(kaug_ref[k_rows, :], qt_aug)
            if t == i:
                s = jnp.where(causal, s, MASKED_SCORE)
            s_ref[slot, k_rows, :] = s
            tile_max = jnp.max(s, axis=0, keepdims=True)
            m = tile_max if m is None else jnp.maximum(m, tile_max)
            yield
        col_max[i] = m

    def value_stage(i):
        slot = i % 2
        m = col_max.pop(i)
        l = None
        for t in range(i + 1):
            k_rows = slice(t * MOBA_BLOCK, (t + 1) * MOBA_BLOCK)
            p = jnp.exp2(s_ref[slot, k_rows, :] - m)
            tile_sum = jnp.sum(p, axis=0, keepdims=True)
            l = tile_sum if l is None else l + tile_sum
            p_ref[slot, k_rows, :] = p.astype(BF16)
            yield
        n_keys = (i + 1) * MOBA_BLOCK
        out_t = _dot(vt_ref[:, :n_keys], p_ref[slot, :n_keys, :]) / l
        a_ref[i * MOBA_BLOCK:(i + 1) * MOBA_BLOCK, :] = out_t.T.astype(BF16)
        yield

    def retention_stage():
        log_g = logg_ref[pl.program_id(1)]
        row = lax.broadcasted_iota(jnp.int32, (RET_CHUNK, RET_CHUNK), 0)
        colm = lax.broadcasted_iota(jnp.int32, (RET_CHUNK, RET_CHUNK), 1)
        rel = (row - colm).astype(F32)
        decay = jnp.where(rel >= 0, jnp.exp(log_g * jnp.maximum(rel, 0.0)), 0.0) * ATTN_SCALE
        pos = lax.broadcasted_iota(jnp.int32, (RET_CHUNK, HEAD_DIM), 0).astype(F32)
        xi = jnp.exp(log_g * (pos + 1.0))
        zeta = jnp.exp(log_g * (RET_CHUNK - 1.0 - pos)) * ATTN_SCALE
        g_chunk = jnp.exp(log_g * RET_CHUNK)
        gn_w = gw_ref[...]
        gn_b = gb_ref[...]
        state = jnp.zeros((HEAD_DIM, HEAD_DIM), F32)
        for c in range(seq // RET_CHUNK):
            rows = slice(c * RET_CHUNK, (c + 1) * RET_CHUNK)
            qc = qr_ref[rows, :]
            kc = kr_ref[rows, :]
            vc = vr_ref[rows, :]
            inner = _dot_nt(qc, kc) * decay
            y = _dot(inner.astype(BF16), vc)
            if c > 0:
                y = y + _dot(qc, state.astype(BF16)) * xi
            kz_t = (kc.astype(F32) * zeta).T.astype(BF16)
            state = state * g_chunk + _dot(kz_t, vc)
            mu = jnp.mean(y, axis=-1, keepdims=True)
            yc = y - mu
            var = jnp.mean(yc * yc, axis=-1, keepdims=True)
            r = yc * lax.rsqrt(var + GN_EPS) * gn_w + gn_b
            g = g_ref[rows, :].astype(F32)
            r_ref[rows, :] = (g * jax.nn.sigmoid(g) * r).astype(BF16)
            yield

    retention, prep, casts = retention_stage(), prep_stage(), cast_stage()
    next(prep)
    next(prep, None)
    for _ in score_stage(0):
        pass
    for i in range(nb):
        stages = [value_stage(i)] + ([score_stage(i + 1)] if i + 1 < nb else [])
        first_pass = True
        while stages:
            for stage in list(stages):
                if next(stage, "done") == "done":
                    stages.remove(stage)
            if first_pass:
                next(prep, None)
                next(retention, None)
                first_pass = False
            next(casts, None)
    for stage in (prep, retention, casts):
        for _ in stage:
            pass


def _mixer(proj3, log_g, gn_w, gn_b, weights_f32):
    batch, seq, _ = proj3.shape
    nb = seq // MOBA_BLOCK
    assert nb <= HEAD_DIM, "block-selection rows must fit in the spare contraction rows"
    n_steps = batch * N_HEADS

    def col(group):
        return pl.BlockSpec((None, seq, HEAD_DIM), lambda b, h, lg: (b, 0, group * N_HEADS + h))

    head_vec = pl.BlockSpec((1, HEAD_DIM), lambda b, h, lg: (0, h))
    head_out = pl.BlockSpec((None, seq, HEAD_DIM), lambda b, h, lg: (b, 0, h))

    cast_in, cast_out, cast_shapes = [], [], []
    for w, col_chunk in weights_f32:
        n_rows, n_cols = w.shape
        rows = n_rows // n_steps
        assert rows * n_steps == n_rows and rows % BF16_SUBLANES == 0
        cast_in.append(pl.BlockSpec((rows, n_cols), lambda b, h, lg: (b * N_HEADS + h, 0)))
        if col_chunk is None:
            cast_out.append(cast_in[-1])
            cast_shapes.append(jax.ShapeDtypeStruct(w.shape, BF16))
        else:
            n_chunks = n_cols // col_chunk
            cast_out.append(pl.BlockSpec((n_chunks, rows, col_chunk),
                                         lambda b, h, lg: (0, b * N_HEADS + h, 0)))
            cast_shapes.append(jax.ShapeDtypeStruct((n_chunks, n_rows, col_chunk), BF16))

    head_shape = jax.ShapeDtypeStruct((batch, seq, GROUP_WIDTH), BF16)
    outs = pl.pallas_call(
        _mixer_kernel,
        grid_spec=pltpu.PrefetchScalarGridSpec(
            num_scalar_prefetch=1,
            grid=(batch, N_HEADS),
            in_specs=[col(0), col(1), col(2), col(3), col(4), col(5), col(6), head_vec, head_vec]
            + cast_in,
            out_specs=[head_out, head_out] + cast_out,
            scratch_shapes=[
                pltpu.VMEM((seq, 2 * HEAD_DIM), BF16),
                pltpu.VMEM((HEAD_DIM, seq), BF16),
                pltpu.VMEM((nb, HEAD_DIM), F32),
                pltpu.VMEM((2, 2 * HEAD_DIM, MOBA_BLOCK), BF16),
                pltpu.VMEM((2, seq, MOBA_BLOCK), F32),
                pltpu.VMEM((2, seq, MOBA_BLOCK), BF16),
            ],
        ),
        out_shape=[head_shape, head_shape] + cast_shapes,
        compiler_params=pltpu.CompilerParams(
            dimension_semantics=("arbitrary", "arbitrary"),
            vmem_limit_bytes=V7X_VMEM_LIMIT_BYTES),
        name="mixer",
    )(log_g, *([proj3] * 7), gn_w, gn_b, *[w for w, _ in weights_f32])
    return outs[0], outs[1], outs[2:]


def _outproj_kernel(a_ref, r_ref, w_ref, x_ref, post_ref, pre_ref, x1_ref, hn_ref):
    for rows in _tapered_tiles(x_ref.shape[0]):
        mix = (_dot(a_ref[rows, :], w_ref[:GROUP_WIDTH, :])
               + _dot(r_ref[rows, :], w_ref[GROUP_WIDTH:, :]))
        x1 = x_ref[rows, :] + mix * _rms_scale(mix) * post_ref[...]
        x1_ref[rows, :] = x1
        hn_ref[rows, :] = (x1 * _rms_scale(x1) * pre_ref[...]).astype(BF16)


def _outproj(a2, r2, w_out_bf16, x2, norm_post, norm_pre, tm):
    tokens, d_model = x2.shape
    row_tile = lambda width: pl.BlockSpec((tm, width), lambda i: (i, 0))
    whole = lambda shape: pl.BlockSpec(shape, lambda i: (0, 0))
    return pl.pallas_call(
        _outproj_kernel,
        grid=(tokens // tm,),
        in_specs=[row_tile(GROUP_WIDTH), row_tile(GROUP_WIDTH), whole(w_out_bf16.shape),
                  row_tile(d_model), whole((1, d_model)), whole((1, d_model))],
        out_specs=[row_tile(d_model), row_tile(d_model)],
        out_shape=[jax.ShapeDtypeStruct((tokens, d_model), F32),
                   jax.ShapeDtypeStruct((tokens, d_model), BF16)],
        compiler_params=pltpu.CompilerParams(
            dimension_semantics=("arbitrary",),
            vmem_limit_bytes=V7X_VMEM_LIMIT_BYTES),
        name="outproj",
    )(a2, r2, w_out_bf16, x2, norm_post, norm_pre)


def _ffn_kernel(hn_ref, wg_ref, wv_ref, cw_ref, cb_ref, wd_ref, x1c_ref, post_ref,
                o_ref, x1_ref, carry_ref, *, tiles_per_seq):
    i = pl.program_id(0)
    j = pl.program_id(1)
    last = pl.num_programs(1) - 1
    tf = wg_ref.shape[1]
    n_x1, _, x1_cols = x1_ref.shape

    @pl.when(i % tiles_per_seq == 0)
    def _():
        carry_ref[j] = jnp.zeros((F32_SUBLANES, tf), F32)

    @pl.when(j < n_x1)
    def _():
        x1_ref[j] = x1c_ref[...]

    def body(first_chunk, last_chunk):
        cw = cw_ref[...]
        cb = cb_ref[...]
        history = carry_ref[j]
        row = lax.broadcasted_iota(jnp.int32, (SUB_ROWS, tf), 0)
        tiles = _sub_tiles(hn_ref.shape[0])
        ug, uv, hidden = {}, {}, {}

        def up(s):
            hn = hn_ref[tiles[s], :]
            ug[s] = _dot(hn, wg_ref[...])
            uv[s] = _dot(hn, wv_ref[...])

        def conv(s):
            if s == 0:
                prev2 = history[F32_SUBLANES - 2:F32_SUBLANES - 1, :]
                prev1 = history[F32_SUBLANES - 1:, :]
            else:
                prev2, prev1 = ug[s - 1][SUB_ROWS - 2:SUB_ROWS - 1, :], ug[s - 1][SUB_ROWS - 1:, :]
            g = ug[s]
            back1 = jnp.where(row == 0, prev1, pltpu.roll(g, 1, axis=0))
            back2 = jnp.where(row == 0, prev2,
                              jnp.where(row == 1, prev1, pltpu.roll(g, 2, axis=0)))
            c = cw[0:1, :] * back2 + cw[1:2, :] * back1 + cw[2:3, :] * g + cb
            hidden[s] = (c * jax.nn.sigmoid(c) * uv.pop(s)).astype(BF16)

        def down(s):
            rows = tiles[s]
            part = _dot(hidden.pop(s), wd_ref[...])
            ff = part if first_chunk else o_ref[rows, :] + part
            if last_chunk:
                normed = ff * _rms_scale(ff) * post_ref[...]
                for c in range(n_x1):
                    cols = slice(c * x1_cols, (c + 1) * x1_cols)
                    o_ref[rows, cols] = x1_ref[c, rows, :] + normed[:, cols]
            else:
                o_ref[rows, :] = ff

        n_sub = len(tiles)
        up(0)
        for s in range(n_sub):
            if s + 1 < n_sub:
                up(s + 1)
            if s >= 1:
                down(s - 1)
            conv(s)
        down(n_sub - 1)
        carry_ref[j] = ug[n_sub - 1][SUB_ROWS - F32_SUBLANES:, :]

    pl.when(j == 0)(lambda: body(True, False))
    pl.when((j > 0) & (j < last))(lambda: body(False, False))
    pl.when(j == last)(lambda: body(False, True))


def _ffn(hn2, w_up_bf16, conv_w, conv_b, w_down_bf16, x1, norm_post, seq, tm, tf):
    tokens, d_model = x1.shape
    d_ff = w_down_bf16.shape[0]
    n_chunks = d_ff // tf
    n_x1 = d_model // X1_SLAB_COLS
    assert n_chunks >= 2 and n_chunks >= n_x1
    row_tile = pl.BlockSpec((tm, d_model), lambda i, j: (i, 0))
    return pl.pallas_call(
        functools.partial(_ffn_kernel, tiles_per_seq=seq // tm),
        grid=(tokens // tm, n_chunks),
        in_specs=[
            row_tile,
            pl.BlockSpec((None, d_model, tf), lambda i, j: (j, 0, 0)),
            pl.BlockSpec((None, d_model, tf), lambda i, j: (n_chunks + j, 0, 0)),
            pl.BlockSpec((CONV_WIDTH, tf), lambda i, j: (0, j)),
            pl.BlockSpec((1, tf), lambda i, j: (0, j)),
            pl.BlockSpec((tf, d_model), lambda i, j: (j, 0)),
            pl.BlockSpec((tm, X1_SLAB_COLS), lambda i, j: (i, jnp.minimum(j, n_x1 - 1))),
            pl.BlockSpec((1, d_model), lambda i, j: (0, 0)),
        ],
        out_specs=row_tile,
        out_shape=jax.ShapeDtypeStruct((tokens, d_model), F32),
        scratch_shapes=[pltpu.VMEM((n_x1, tm, X1_SLAB_COLS), F32),
                        pltpu.VMEM((n_chunks, F32_SUBLANES, tf), F32)],
        compiler_params=pltpu.CompilerParams(
            dimension_semantics=("arbitrary", "arbitrary"),
            vmem_limit_bytes=V7X_VMEM_LIMIT_BYTES),
        name="ffn",
    )(hn2, w_up_bf16, w_up_bf16, conv_w, conv_b, w_down_bf16, x1, norm_post)


def _rope_tables(seq):
    pos = np.arange(seq, dtype=np.float32)
    inv = (1.0 / (ROPE_THETA ** (np.arange(0, HEAD_DIM, 2, dtype=np.float32) / HEAD_DIM))).astype(np.float32)
    ang = pos[:, None] * inv[None, :]
    cos, sin = np.cos(ang), np.sin(ang)
    cos_full = np.concatenate([cos, cos], axis=-1)
    sin_signed = np.concatenate([-sin, sin], axis=-1)
    q_scale = np.float32(MOBA_Q_SCALE)
    tab_a = np.stack([cos_full * q_scale, cos_full])
    tab_b = np.stack([sin_signed * q_scale, sin_signed])
    return jnp.asarray(tab_a, F32), jnp.asarray(tab_b, F32)


def _retention_log_decay():
    heads = np.arange(N_HEADS, dtype=np.float32)
    return jnp.asarray(np.log(np.float32(1.0) - np.float32(2.0) ** (-5.0 - heads)), F32)


def _tile(n, preferred):
    for t in preferred:
        if n % t == 0:
            return t
    raise ValueError(f"no supported tile for extent {n}")


def kernel(x, norm_mix_pre, w_in, ret_gn_w, ret_gn_b, w_out, norm_mix_post, norm_ffn_pre, w_up, conv_w, conv_b, w_down, norm_ffn_post):
    batch, seq, d_model = x.shape
    depth = w_in.shape[0]
    assert seq % MOBA_BLOCK == 0 and seq % RET_CHUNK == 0
    assert w_in.shape[2] == 7 * GROUP_WIDTH and w_out.shape[1] == 2 * GROUP_WIDTH
    d_ff = w_down.shape[1]
    tokens = batch * seq
    tm_proj = _tile(seq, (256,))
    tm_out = _tile(seq, (512, 256))
    tm_ffn = _tile(seq, (1024, 512, 256))
    tf = _tile(d_ff, (512, 256, 128))

    tab_a, tab_b = _rope_tables(seq)
    log_g = _retention_log_decay()

    x2 = x.reshape(tokens, d_model)
    for layer in range(depth):
        proj = _inproj(x2, norm_mix_pre[layer][None, :], w_in[layer].astype(BF16),
                       tab_a, tab_b, seq, tm_proj)
        proj3 = proj.reshape(batch, seq, -1)
        a, r, (w_out_bf16, w_up_bf16, w_down_bf16) = _mixer(
            proj3, log_g, ret_gn_w[layer][None, :], ret_gn_b[layer][None, :],
            ((w_out[layer], None), (w_up[layer], tf), (w_down[layer], None)))
        x1, hn2 = _outproj(a.reshape(tokens, GROUP_WIDTH), r.reshape(tokens, GROUP_WIDTH),
                           w_out_bf16, x2,
                           norm_mix_post[layer][None, :], norm_ffn_pre[layer][None, :], tm_out)
        x2 = _ffn(hn2, w_up_bf16, conv_w[layer], conv_b[layer][None, :],
                  w_down_bf16, x1, norm_ffn_post[layer][None, :], seq, tm_ffn, tf)
    return x2.reshape(batch, seq, d_model)
```

```python
import functools

import jax
import jax.numpy as jnp
import numpy as np
from jax import lax
from jax.experimental import pallas as pl
from jax.experimental.pallas import tpu as pltpu

HEAD_DIM = 128
N_HEADS = 8
GROUP_WIDTH = N_HEADS * HEAD_DIM
MOBA_BLOCK = 256
MOBA_TOPK = 3
RET_CHUNK = 256
ROPE_THETA = 10000.0
CONV_WIDTH = 3
NORM_EPS = 1e-6
GN_EPS = 1e-5
ATTN_SCALE = HEAD_DIM ** -0.5
MASKED_SCORE = -1e30
MOBA_Q_SCALE = ATTN_SCALE * 1.4426950408889634

V7X_VMEM_LIMIT_BYTES = 56 * 1024 * 1024
SUB_ROWS = 256
X1_SLAB_COLS = 256
F32_SUBLANES = 8
BF16_SUBLANES = 16

F32 = jnp.float32
BF16 = jnp.bfloat16


def _dot(a, b):
    return jnp.dot(a, b, preferred_element_type=F32)


def _dot_nt(a, b):
    return lax.dot_general(a, b, (((1,), (1,)), ((), ())), preferred_element_type=F32)


def _rms_scale(v):
    return lax.rsqrt(jnp.mean(v * v, axis=-1, keepdims=True) + NORM_EPS)


def _sub_tiles(rows):
    return [slice(r, r + SUB_ROWS) for r in range(0, rows, SUB_ROWS)]


ROTARY_SCALED, ROTARY, PLAIN = 0, 1, 2
INPROJ_GROUP_KIND = (ROTARY_SCALED, ROTARY, PLAIN, ROTARY, ROTARY, PLAIN, PLAIN)


def _inproj_kernel(x_ref, nw_ref, w_ref, a_ref, b_ref, o_ref):
    x = x_ref[...]
    hn = (x * _rms_scale(x) * nw_ref[...]).astype(BF16)
    acc = {}

    def project(g):
        acc[g] = _dot(hn, w_ref[:, g * GROUP_WIDTH:(g + 1) * GROUP_WIDTH])

    def finish(g):
        kind = INPROJ_GROUP_KIND[g]
        t_all = acc.pop(g)
        if kind != PLAIN:
            a = a_ref[kind]
            b = b_ref[kind]
        for h in range(N_HEADS):
            t = t_all[:, h * HEAD_DIM:(h + 1) * HEAD_DIM]
            if kind != PLAIN:
                t = t * a + pltpu.roll(t, HEAD_DIM // 2, axis=1) * b
            o_ref[g * N_HEADS + h] = t.astype(BF16)

    n_groups = len(INPROJ_GROUP_KIND)
    project(0)
    for g in range(n_groups):
        if g + 1 < n_groups:
            project(g + 1)
        finish(g)


def _inproj(x2, norm_w, w_in_bf16, tab_a, tab_b, seq, tm):
    tokens, d_model = x2.shape
    in_cols = w_in_bf16.shape[1]
    assert in_cols == len(INPROJ_GROUP_KIND) * GROUP_WIDTH
    tiles_per_seq = seq // tm
    table = pl.BlockSpec((tab_a.shape[0], tm, HEAD_DIM), lambda i: (0, i % tiles_per_seq, 0))
    return pl.pallas_call(
        _inproj_kernel,
        grid=(tokens // tm,),
        in_specs=[
            pl.BlockSpec((tm, d_model), lambda i: (i, 0)),
            pl.BlockSpec((1, d_model), lambda i: (0, 0)),
            pl.BlockSpec((d_model, in_cols), lambda i: (0, 0), pipeline_mode=pl.Buffered(1)),
            table,
            table,
        ],
        out_specs=pl.BlockSpec((in_cols // HEAD_DIM, tm, HEAD_DIM), lambda i: (0, i, 0)),
        out_shape=jax.ShapeDtypeStruct((in_cols // HEAD_DIM, tokens, HEAD_DIM), BF16),
        compiler_params=pltpu.CompilerParams(
            dimension_semantics=("arbitrary",),
            vmem_limit_bytes=V7X_VMEM_LIMIT_BYTES),
        name="inproj",
    )(x2, norm_w, w_in_bf16, tab_a, tab_b)


def _mixer_kernel(logg_ref, q_ref, k_ref, v_ref, qr_ref, kr_ref, vr_ref, g_ref, gw_ref, gb_ref, *rest):
    n_cast = (len(rest) - 8) // 2
    cast_in, a_ref, r_ref = rest[:n_cast], rest[n_cast], rest[n_cast + 1]
    cast_out = rest[n_cast + 2:2 * n_cast + 2]
    kaug_ref, vt_ref, kmean_ref, qt_ref, s_ref, p_ref = rest[2 * n_cast + 2:]
    seq = q_ref.shape[0]
    nb = seq // MOBA_BLOCK

    for src, dst in zip(cast_in, cast_out):
        if len(dst.shape) == 2:
            dst[...] = src[...].astype(BF16)
        else:
            width = dst.shape[2]
            for c in range(dst.shape[0]):
                dst[c] = src[:, c * width:(c + 1) * width].astype(BF16)

    lane = lax.broadcasted_iota(jnp.int32, (MOBA_BLOCK, HEAD_DIM), 1)
    for jb in range(nb):
        rows = slice(jb * MOBA_BLOCK, (jb + 1) * MOBA_BLOCK)
        kb = k_ref[rows, :]
        kaug_ref[rows, :HEAD_DIM] = kb
        kaug_ref[rows, HEAD_DIM:] = jnp.where(lane == jb, 1.0, 0.0).astype(BF16)
        vt_ref[:, rows] = v_ref[rows, :].astype(F32).T.astype(BF16)
        kmean_ref[jb:jb + 1, :] = jnp.mean(kb.astype(F32), axis=0, keepdims=True)
    for slot in range(2):
        qt_ref[slot, HEAD_DIM:, :] = jnp.zeros((HEAD_DIM, MOBA_BLOCK), BF16)
    kmean = kmean_ref[...].astype(BF16)

    blk_idx = lax.broadcasted_iota(jnp.int32, (nb, MOBA_BLOCK), 0)
    key_pos = lax.broadcasted_iota(jnp.int32, (MOBA_BLOCK, MOBA_BLOCK), 0)
    qry_pos = lax.broadcasted_iota(jnp.int32, (MOBA_BLOCK, MOBA_BLOCK), 1)
    causal = key_pos <= qry_pos

    col_max = {}

    def score_stage(i):
        slot = i % 2
        qt = q_ref[i * MOBA_BLOCK:(i + 1) * MOBA_BLOCK, :].astype(F32).T.astype(BF16)
        past = blk_idx < i
        gate = jnp.where(past, _dot(kmean, qt), -jnp.inf)
        beaten_by = jnp.zeros((nb, MOBA_BLOCK), jnp.int32)
        for jp in range(i):
            row = gate[jp:jp + 1, :]
            beats = (row > gate) | ((row == gate) & (jp < blk_idx))
            beaten_by = beaten_by + beats.astype(jnp.int32)
        visible = (past & (beaten_by < MOBA_TOPK)) | (blk_idx == i)
        qt_ref[slot, :HEAD_DIM, :] = qt
        qt_ref[slot, HEAD_DIM:HEAD_DIM + nb, :] = jnp.where(visible, 0.0, MASKED_SCORE).astype(BF16)
        qt_aug = qt_ref[slot]
        yield
        m = None
        for t in range(i + 1):
            k_rows = slice(t * MOBA_BLOCK, (t + 1) * MOBA_BLOCK)
            s = _dot(kaug_ref[k_rows, :], qt_aug)
            if t == i:
                s = jnp.where(causal, s, MASKED_SCORE)
            s_ref[slot, k_rows, :] = s
            tile_max = jnp.max(s, axis=0, keepdims=True)
            m = tile_max if m is None else jnp.maximum(m, tile_max)
            yield
        col_max[i] = m

    def value_stage(i):
        slot = i % 2
        m = col_max.pop(i)
        l = None
        for t in range(i + 1):
            k_rows = slice(t * MOBA_BLOCK, (t + 1) * MOBA_BLOCK)
            p = jnp.exp2(s_ref[slot, k_rows, :] - m)
            tile_sum = jnp.sum(p, axis=0, keepdims=True)
            l = tile_sum if l is None else l + tile_sum
            p_ref[slot, k_rows, :] = p.astype(BF16)
            yield
        n_keys = (i + 1) * MOBA_BLOCK
        out_t = _dot(vt_ref[:, :n_keys], p_ref[slot, :n_keys, :]) / l
        a_ref[i * MOBA_BLOCK:(i + 1) * MOBA_BLOCK, :] = out_t.T.astype(BF16)
        yield

    def retention_stage():
        log_g = logg_ref[pl.program_id(1)]
        row = lax.broadcasted_iota(jnp.int32, (RET_CHUNK, RET_CHUNK), 0)
        colm = lax.broadcasted_iota(jnp.int32, (RET_CHUNK, RET_CHUNK), 1)
        rel = (row - colm).astype(F32)
        decay = jnp.where(rel >= 0, jnp.exp(log_g * jnp.maximum(rel, 0.0)), 0.0) * ATTN_SCALE
        pos = lax.broadcasted_iota(jnp.int32, (RET_CHUNK, HEAD_DIM), 0).astype(F32)
        xi = jnp.exp(log_g * (pos + 1.0))
        zeta = jnp.exp(log_g * (RET_CHUNK - 1.0 - pos)) * ATTN_SCALE
        g_chunk = jnp.exp(log_g * RET_CHUNK)
        gn_w = gw_ref[...]
        gn_b = gb_ref[...]
        state = jnp.zeros((HEAD_DIM, HEAD_DIM), F32)
        for c in range(seq // RET_CHUNK):
            rows = slice(c * RET_CHUNK, (c + 1) * RET_CHUNK)
            qc = qr_ref[rows, :]
            kc = kr_ref[rows, :]
            vc = vr_ref[rows, :]
            inner = _dot_nt(qc, kc) * decay
            y = _dot(inner.astype(BF16), vc)
            if c > 0:
                y = y + _dot(qc, state.astype(BF16)) * xi
            kz_t = (kc.astype(F32) * zeta).T.astype(BF16)
            state = state * g_chunk + _dot(kz_t, vc)
            mu = jnp.mean(y, axis=-1, keepdims=True)
            yc = y - mu
            var = jnp.mean(yc * yc, axis=-1, keepdims=True)
            r = yc * lax.rsqrt(var + GN_EPS) * gn_w + gn_b
            g = g_ref[rows, :].astype(F32)
            r_ref[rows, :] = (g * jax.nn.sigmoid(g) * r).astype(BF16)
            yield

    retention = retention_stage()
    for _ in score_stage(0):
        pass
    for i in range(nb):
        stages = [value_stage(i)] + ([score_stage(i + 1)] if i + 1 < nb else [])
        first_pass = True
        while stages:
            for stage in list(stages):
                if next(stage, "done") == "done":
                    stages.remove(stage)
            if first_pass:
                next(retention, None)
                first_pass = False
    for _ in retention:
        pass


def _mixer(proj_hm, batch, log_g, gn_w, gn_b, weights_f32):
    tokens = proj_hm.shape[1]
    seq = tokens // batch
    nb = seq // MOBA_BLOCK
    assert nb <= HEAD_DIM, "block-selection rows must fit in the spare contraction rows"
    n_steps = batch * N_HEADS

    def col(group):
        return pl.BlockSpec((None, seq, HEAD_DIM), lambda b, h, lg: (group * N_HEADS + h, b, 0))

    head_vec = pl.BlockSpec((1, HEAD_DIM), lambda b, h, lg: (0, h))
    head_out = pl.BlockSpec((None, seq, HEAD_DIM), lambda b, h, lg: (h, b, 0))

    cast_in, cast_out, cast_shapes = [], [], []
    for w, col_chunk in weights_f32:
        n_rows, n_cols = w.shape
        rows = n_rows // n_steps
        assert rows * n_steps == n_rows and rows % BF16_SUBLANES == 0
        cast_in.append(pl.BlockSpec((rows, n_cols), lambda b, h, lg: (b * N_HEADS + h, 0)))
        if col_chunk is None:
            cast_out.append(cast_in[-1])
            cast_shapes.append(jax.ShapeDtypeStruct(w.shape, BF16))
        else:
            n_chunks = n_cols // col_chunk
            cast_out.append(pl.BlockSpec((n_chunks, rows, col_chunk),
                                         lambda b, h, lg: (0, b * N_HEADS + h, 0)))
            cast_shapes.append(jax.ShapeDtypeStruct((n_chunks, n_rows, col_chunk), BF16))

    head_shape = jax.ShapeDtypeStruct((N_HEADS, tokens, HEAD_DIM), BF16)
    outs = pl.pallas_call(
        _mixer_kernel,
        grid_spec=pltpu.PrefetchScalarGridSpec(
            num_scalar_prefetch=1,
            grid=(batch, N_HEADS),
            in_specs=[col(0), col(1), col(2), col(3), col(4), col(5), col(6), head_vec, head_vec]
            + cast_in,
            out_specs=[head_out, head_out] + cast_out,
            scratch_shapes=[
                pltpu.VMEM((seq, 2 * HEAD_DIM), BF16),
                pltpu.VMEM((HEAD_DIM, seq), BF16),
                pltpu.VMEM((nb, HEAD_DIM), F32),
                pltpu.VMEM((2, 2 * HEAD_DIM, MOBA_BLOCK), BF16),
                pltpu.VMEM((2, seq, MOBA_BLOCK), F32),
                pltpu.VMEM((2, seq, MOBA_BLOCK), BF16),
            ],
        ),
        out_shape=[head_shape, head_shape] + cast_shapes,
        compiler_params=pltpu.CompilerParams(
            dimension_semantics=("arbitrary", "arbitrary"),
            vmem_limit_bytes=V7X_VMEM_LIMIT_BYTES),
        name="mixer",
    )(log_g, *([proj_hm] * 7), gn_w, gn_b, *[w for w, _ in weights_f32])
    return outs[0], outs[1], outs[2:]


def _outproj_kernel(a_ref, r_ref, w_ref, x_ref, post_ref, pre_ref, x1_ref, hn_ref):
    def heads_side_by_side(ref, rows):
        return jnp.concatenate([ref[h, rows, :] for h in range(N_HEADS)], axis=-1)

    for rows in _sub_tiles(x_ref.shape[0]):
        mix = (_dot(heads_side_by_side(a_ref, rows), w_ref[:GROUP_WIDTH, :])
               + _dot(heads_side_by_side(r_ref, rows), w_ref[GROUP_WIDTH:, :]))
        x1 = x_ref[rows, :] + mix * _rms_scale(mix) * post_ref[...]
        x1_ref[rows, :] = x1
        hn_ref[rows, :] = (x1 * _rms_scale(x1) * pre_ref[...]).astype(BF16)


def _outproj(a2, r2, w_out_bf16, x2, norm_post, norm_pre, tm):
    tokens, d_model = x2.shape
    row_tile = lambda width: pl.BlockSpec((tm, width), lambda i: (i, 0))
    whole = lambda shape: pl.BlockSpec(shape, lambda i: (0, 0))
    head_tile = pl.BlockSpec((N_HEADS, tm, HEAD_DIM), lambda i: (0, i, 0))
    return pl.pallas_call(
        _outproj_kernel,
        grid=(tokens // tm,),
        in_specs=[head_tile, head_tile, whole(w_out_bf16.shape),
                  row_tile(d_model), whole((1, d_model)), whole((1, d_model))],
        out_specs=[row_tile(d_model), row_tile(d_model)],
        out_shape=[jax.ShapeDtypeStruct((tokens, d_model), F32),
                   jax.ShapeDtypeStruct((tokens, d_model), BF16)],
        compiler_params=pltpu.CompilerParams(
            dimension_semantics=("arbitrary",),
            vmem_limit_bytes=V7X_VMEM_LIMIT_BYTES),
        name="outproj",
    )(a2, r2, w_out_bf16, x2, norm_post, norm_pre)


def _ffn_kernel(hn_ref, wg_ref, wv_ref, cw_ref, cb_ref, wd_ref, x1c_ref, post_ref,
                o_ref, x1_ref, carry_ref, *, tiles_per_seq):
    i = pl.program_id(0)
    j = pl.program_id(1)
    last = pl.num_programs(1) - 1
    tf = wg_ref.shape[1]
    n_x1, _, x1_cols = x1_ref.shape

    @pl.when(i % tiles_per_seq == 0)
    def _():
        carry_ref[j] = jnp.zeros((F32_SUBLANES, tf), F32)

    @pl.when(j < n_x1)
    def _():
        x1_ref[j] = x1c_ref[...]

    def body(first_chunk, last_chunk):
        cw = cw_ref[...]
        cb = cb_ref[...]
        history = carry_ref[j]
        row = lax.broadcasted_iota(jnp.int32, (SUB_ROWS, tf), 0)
        tiles = _sub_tiles(hn_ref.shape[0])
        ug, uv, hidden = {}, {}, {}

        def up(s):
            hn = hn_ref[tiles[s], :]
            ug[s] = _dot(hn, wg_ref[...])
            uv[s] = _dot(hn, wv_ref[...])

        def conv(s):
            if s == 0:
                prev2 = history[F32_SUBLANES - 2:F32_SUBLANES - 1, :]
                prev1 = history[F32_SUBLANES - 1:, :]
            else:
                prev2, prev1 = ug[s - 1][SUB_ROWS - 2:SUB_ROWS - 1, :], ug[s - 1][SUB_ROWS - 1:, :]
            g = ug[s]
            back1 = jnp.where(row == 0, prev1, pltpu.roll(g, 1, axis=0))
            back2 = jnp.where(row == 0, prev2,
                              jnp.where(row == 1, prev1, pltpu.roll(g, 2, axis=0)))
            c = cw[0:1, :] * back2 + cw[1:2, :] * back1 + cw[2:3, :] * g + cb
            hidden[s] = (c * jax.nn.sigmoid(c) * uv.pop(s)).astype(BF16)

        def down(s):
            rows = tiles[s]
            part = _dot(hidden.pop(s), wd_ref[...])
            ff = part if first_chunk else o_ref[rows, :] + part
            if last_chunk:
                normed = ff * _rms_scale(ff) * post_ref[...]
                for c in range(n_x1):
                    cols = slice(c * x1_cols, (c + 1) * x1_cols)
                    o_ref[rows, cols] = x1_ref[c, rows, :] + normed[:, cols]
            else:
                o_ref[rows, :] = ff

        n_sub = len(tiles)
        up(0)
        for s in range(n_sub):
            if s + 1 < n_sub:
                up(s + 1)
            if s >= 1:
                down(s - 1)
            conv(s)
        down(n_sub - 1)
        carry_ref[j] = ug[n_sub - 1][SUB_ROWS - F32_SUBLANES:, :]

    pl.when(j == 0)(lambda: body(True, False))
    pl.when((j > 0) & (j < last))(lambda: body(False, False))
    pl.when(j == last)(lambda: body(False, True))


def _ffn(hn2, w_up_bf16, conv_w, conv_b, w_down_bf16, x1, norm_post, seq, tm, tf):
    tokens, d_model = x1.shape
    d_ff = w_down_bf16.shape[0]
    n_chunks = d_ff // tf
    n_x1 = d_model // X1_SLAB_COLS
    assert n_chunks >= 2 and n_chunks >= n_x1
    row_tile = pl.BlockSpec((tm, d_model), lambda i, j: (i, 0))
    return pl.pallas_call(
        functools.partial(_ffn_kernel, tiles_per_seq=seq // tm),
        grid=(tokens // tm, n_chunks),
        in_specs=[
            row_tile,
            pl.BlockSpec((None, d_model, tf), lambda i, j: (j, 0, 0)),
            pl.BlockSpec((None, d_model, tf), lambda i, j: (n_chunks + j, 0, 0)),
            pl.BlockSpec((CONV_WIDTH, tf), lambda i, j: (0, j)),
            pl.BlockSpec((1, tf), lambda i, j: (0, j)),
            pl.BlockSpec((tf, d_model), lambda i, j: (j, 0)),
            pl.BlockSpec((tm, X1_SLAB_COLS), lambda i, j: (i, jnp.minimum(j, n_x1 - 1))),
            pl.BlockSpec((1, d_model), lambda i, j: (0, 0)),
        ],
        out_specs=row_tile,
        out_shape=jax.ShapeDtypeStruct((tokens, d_model), F32),
        scratch_shapes=[pltpu.VMEM((n_x1, tm, X1_SLAB_COLS), F32),
                        pltpu.VMEM((n_chunks, F32_SUBLANES, tf), F32)],
        compiler_params=pltpu.CompilerParams(
            dimension_semantics=("arbitrary", "arbitrary"),
            vmem_limit_bytes=V7X_VMEM_LIMIT_BYTES),
        name="ffn",
    )(hn2, w_up_bf16, w_up_bf16, conv_w, conv_b, w_down_bf16, x1, norm_post)


def _rope_tables(seq):
    pos = np.arange(seq, dtype=np.float32)
    inv = (1.0 / (ROPE_THETA ** (np.arange(0, HEAD_DIM, 2, dtype=np.float32) / HEAD_DIM))).astype(np.float32)
    ang = pos[:, None] * inv[None, :]
    cos, sin = np.cos(ang), np.sin(ang)
    cos_full = np.concatenate([cos, cos], axis=-1)
    sin_signed = np.concatenate([-sin, sin], axis=-1)
    q_scale = np.float32(MOBA_Q_SCALE)
    tab_a = np.stack([cos_full * q_scale, cos_full])
    tab_b = np.stack([sin_signed * q_scale, sin_signed])
    return jnp.asarray(tab_a, F32), jnp.asarray(tab_b, F32)


def _retention_log_decay():
    heads = np.arange(N_HEADS, dtype=np.float32)
    return jnp.asarray(np.log(np.float32(1.0) - np.float32(2.0) ** (-5.0 - heads)), F32)


def _tile(n, preferred):
    for t in preferred:
        if n % t == 0:
            return t
    raise ValueError(f"no supported tile for extent {n}")


def kernel(x, norm_mix_pre, w_in, ret_gn_w, ret_gn_b, w_out, norm_mix_post, norm_ffn_pre, w_up, conv_w, conv_b, w_down, norm_ffn_post):
    batch, seq, d_model = x.shape
    depth = w_in.shape[0]
    assert seq % MOBA_BLOCK == 0 and seq % RET_CHUNK == 0
    assert w_in.shape[2] == 7 * GROUP_WIDTH and w_out.shape[1] == 2 * GROUP_WIDTH
    d_ff = w_down.shape[1]
    tokens = batch * seq
    tm_proj = _tile(seq, (256,))
    tm_out = _tile(seq, (512, 256))
    tm_ffn = _tile(seq, (1024, 512, 256))
    tf = _tile(d_ff, (512, 256, 128))

    tab_a, tab_b = _rope_tables(seq)
    log_g = _retention_log_decay()

    x2 = x.reshape(tokens, d_model)
    for layer in range(depth):
        proj = _inproj(x2, norm_mix_pre[layer][None, :], w_in[layer].astype(BF16),
                       tab_a, tab_b, seq, tm_proj)
        a, r, (w_out_bf16, w_up_bf16, w_down_bf16) = _mixer(
            proj, batch, log_g, ret_gn_w[layer][None, :], ret_gn_b[layer][None, :],
            ((w_out[layer], None), (w_up[layer], tf), (w_down[layer], None)))
        x1, hn2 = _outproj(a, r, w_out_bf16, x2,
                           norm_mix_post[layer][None, :], norm_ffn_pre[layer][None, :], tm_out)
        x2 = _ffn(hn2, w_up_bf16, conv_w[layer], conv_b[layer][None, :],
                  w_down_bf16, x1, norm_ffn_post[layer][None, :], seq, tm_ffn, tf)
    return x2.reshape(batch, seq, d_model)
```

```python
import functools

import jax
import jax.numpy as jnp
import numpy as np
from jax import lax
from jax.experimental import pallas as pl
from jax.experimental.pallas import tpu as pltpu

HEAD_DIM = 128
N_HEADS = 8
GROUP_WIDTH = N_HEADS * HEAD_DIM
MOBA_BLOCK = 256
MOBA_TOPK = 3
RET_CHUNK = 256
ROPE_THETA = 10000.0
CONV_WIDTH = 3
NORM_EPS = 1e-6
GN_EPS = 1e-5
ATTN_SCALE = HEAD_DIM ** -0.5
MASKED_SCORE = -1e30
MOBA_Q_SCALE = ATTN_SCALE * 1.4426950408889634

V7X_VMEM_LIMIT_BYTES = 56 * 1024 * 1024
SUB_ROWS = 256
X1_SLAB_COLS = 256
F32_SUBLANES = 8
BF16_SUBLANES = 16

F32 = jnp.float32
BF16 = jnp.bfloat16


def _dot(a, b):
    return jnp.dot(a, b, preferred_element_type=F32)


def _dot_nt(a, b):
    return lax.dot_general(a, b, (((1,), (1,)), ((), ())), preferred_element_type=F32)


def _rms_scale(v):
    return lax.rsqrt(jnp.mean(v * v, axis=-1, keepdims=True) + NORM_EPS)


def _sub_tiles(rows):
    return [slice(r, r + SUB_ROWS) for r in range(0, rows, SUB_ROWS)]


ROTARY_SCALED, ROTARY, PLAIN = 0, 1, 2
INPROJ_GROUP_KIND = (ROTARY_SCALED, ROTARY, PLAIN, ROTARY, ROTARY, PLAIN, PLAIN)


def _inproj_kernel(x_ref, nw_ref, w_ref, a_ref, b_ref, o_ref):
    x = x_ref[...]
    hn = (x * _rms_scale(x) * nw_ref[...]).astype(BF16)
    acc = {}

    def project(g):
        acc[g] = _dot(hn, w_ref[:, g * GROUP_WIDTH:(g + 1) * GROUP_WIDTH])

    def finish(g):
        kind = INPROJ_GROUP_KIND[g]
        t_all = acc.pop(g)
        if kind == PLAIN:
            o_ref[:, g * GROUP_WIDTH:(g + 1) * GROUP_WIDTH] = t_all.astype(BF16)
            return
        a = a_ref[kind]
        b = b_ref[kind]
        for h in range(N_HEADS):
            t = t_all[:, h * HEAD_DIM:(h + 1) * HEAD_DIM]
            rot = pltpu.roll(t, HEAD_DIM // 2, axis=1)
            cols = slice(g * GROUP_WIDTH + h * HEAD_DIM, g * GROUP_WIDTH + (h + 1) * HEAD_DIM)
            o_ref[:, cols] = (t * a + rot * b).astype(BF16)

    n_groups = len(INPROJ_GROUP_KIND)
    project(0)
    for g in range(n_groups):
        if g + 1 < n_groups:
            project(g + 1)
        finish(g)


def _inproj(x2, norm_w, w_in_bf16, tab_a, tab_b, seq, tm):
    tokens, d_model = x2.shape
    in_cols = w_in_bf16.shape[1]
    assert in_cols == len(INPROJ_GROUP_KIND) * GROUP_WIDTH
    tiles_per_seq = seq // tm
    table = pl.BlockSpec((tab_a.shape[0], tm, HEAD_DIM), lambda i: (0, i % tiles_per_seq, 0))
    return pl.pallas_call(
        _inproj_kernel,
        grid=(tokens // tm,),
        in_specs=[
            pl.BlockSpec((tm, d_model), lambda i: (i, 0)),
            pl.BlockSpec((1, d_model), lambda i: (0, 0)),
            pl.BlockSpec((d_model, in_cols), lambda i: (0, 0), pipeline_mode=pl.Buffered(1)),
            table,
            table,
        ],
        out_specs=pl.BlockSpec((tm, in_cols), lambda i: (i, 0)),
        out_shape=jax.ShapeDtypeStruct((tokens, in_cols), BF16),
        compiler_params=pltpu.CompilerParams(
            dimension_semantics=("arbitrary",),
            vmem_limit_bytes=V7X_VMEM_LIMIT_BYTES),
        name="inproj",
    )(x2, norm_w, w_in_bf16, tab_a, tab_b)


def _mixer_kernel(logg_ref, q_ref, k_ref, v_ref, qr_ref, kr_ref, vr_ref, g_ref, gw_ref, gb_ref, *rest):
    n_cast = (len(rest) - 8) // 2
    cast_in, a_ref, r_ref = rest[:n_cast], rest[n_cast], rest[n_cast + 1]
    cast_out = rest[n_cast + 2:2 * n_cast + 2]
    kaug_ref, vt_ref, kmean_ref, qt_ref, s_ref, p_ref = rest[2 * n_cast + 2:]
    seq = q_ref.shape[0]
    nb = seq // MOBA_BLOCK

    for src, dst in zip(cast_in, cast_out):
        if len(dst.shape) == 2:
            dst[...] = src[...].astype(BF16)
        else:
            width = dst.shape[2]
            for c in range(dst.shape[0]):
                dst[c] = src[:, c * width:(c + 1) * width].astype(BF16)

    lane = lax.broadcasted_iota(jnp.int32, (MOBA_BLOCK, HEAD_DIM), 1)
    for jb in range(nb):
        rows = slice(jb * MOBA_BLOCK, (jb + 1) * MOBA_BLOCK)
        kb = k_ref[rows, :]
        kaug_ref[rows, :HEAD_DIM] = kb
        kaug_ref[rows, HEAD_DIM:] = jnp.where(lane == jb, 1.0, 0.0).astype(BF16)
        vt_ref[:, rows] = v_ref[rows, :].astype(F32).T.astype(BF16)
        kmean_ref[jb:jb + 1, :] = jnp.mean(kb.astype(F32), axis=0, keepdims=True)
    for slot in range(2):
        qt_ref[slot, HEAD_DIM:, :] = jnp.zeros((HEAD_DIM, MOBA_BLOCK), BF16)
    kmean = kmean_ref[...].astype(BF16)

    blk_idx = lax.broadcasted_iota(jnp.int32, (nb, MOBA_BLOCK), 0)
    key_pos = lax.broadcasted_iota(jnp.int32, (MOBA_BLOCK, MOBA_BLOCK), 0)
    qry_pos = lax.broadcasted_iota(jnp.int32, (MOBA_BLOCK, MOBA_BLOCK), 1)
    causal = key_pos <= qry_pos

    col_max = {}

    def score_stage(i):
        slot = i % 2
        qt = q_ref[i * MOBA_BLOCK:(i + 1) * MOBA_BLOCK, :].astype(F32).T.astype(BF16)
        past = blk_idx < i
        gate = jnp.where(past, _dot(kmean, qt), -jnp.inf)
        beaten_by = jnp.zeros((nb, MOBA_BLOCK), jnp.int32)
        for jp in range(i):
            row = gate[jp:jp + 1, :]
            beats = (row > gate) | ((row == gate) & (jp < blk_idx))
            beaten_by = beaten_by + beats.astype(jnp.int32)
        visible = (past & (beaten_by < MOBA_TOPK)) | (blk_idx == i)
        qt_ref[slot, :HEAD_DIM, :] = qt
        qt_ref[slot, HEAD_DIM:HEAD_DIM + nb, :] = jnp.where(visible, 0.0, MASKED_SCORE).astype(BF16)
        qt_aug = qt_ref[slot]
        yield
        m = None
        for t in range(i + 1):
            k_rows = slice(t * MOBA_BLOCK, (t + 1) * MOBA_BLOCK)
            s = _dot(kaug_ref[k_rows, :], qt_aug)
            if t == i:
                s = jnp.where(causal, s, MASKED_SCORE)
            s_ref[slot, k_rows, :] = s
            tile_max = jnp.max(s.reshape(-1, F32_SUBLANES, MOBA_BLOCK), axis=0)
            m = tile_max if m is None else jnp.maximum(m, tile_max)
            yield
        col_max[i] = jnp.max(m, axis=0, keepdims=True)

    def value_stage(i):
        slot = i % 2
        m = col_max.pop(i)
        l = None
        for t in range(i + 1):
            k_rows = slice(t * MOBA_BLOCK, (t + 1) * MOBA_BLOCK)
            p = jnp.exp2(s_ref[slot, k_rows, :] - m)
            tile_sum = jnp.sum(p.reshape(-1, F32_SUBLANES, MOBA_BLOCK), axis=0)
            l = tile_sum if l is None else l + tile_sum
            p_ref[slot, k_rows, :] = p.astype(BF16)
            yield
        n_keys = (i + 1) * MOBA_BLOCK
        denom = jnp.sum(l, axis=0, keepdims=True)
        out_t = _dot(vt_ref[:, :n_keys], p_ref[slot, :n_keys, :]) / denom
        a_ref[i * MOBA_BLOCK:(i + 1) * MOBA_BLOCK, :] = out_t.T.astype(BF16)
        yield

    def retention_stage():
        log_g = logg_ref[pl.program_id(1)]
        row = lax.broadcasted_iota(jnp.int32, (RET_CHUNK, RET_CHUNK), 0)
        colm = lax.broadcasted_iota(jnp.int32, (RET_CHUNK, RET_CHUNK), 1)
        rel = (row - colm).astype(F32)
        decay = jnp.where(rel >= 0, jnp.exp(log_g * jnp.maximum(rel, 0.0)), 0.0) * ATTN_SCALE
        pos = lax.broadcasted_iota(jnp.int32, (RET_CHUNK, HEAD_DIM), 0).astype(F32)
        xi = jnp.exp(log_g * (pos + 1.0))
        zeta = jnp.exp(log_g * (RET_CHUNK - 1.0 - pos)) * ATTN_SCALE
        g_chunk = jnp.exp(log_g * RET_CHUNK)
        gn_w = gw_ref[...]
        gn_b = gb_ref[...]
        state = jnp.zeros((HEAD_DIM, HEAD_DIM), F32)
        for c in range(seq // RET_CHUNK):
            rows = slice(c * RET_CHUNK, (c + 1) * RET_CHUNK)
            qc = qr_ref[rows, :]
            kc = kr_ref[rows, :]
            vc = vr_ref[rows, :]
            inner = _dot_nt(qc, kc) * decay
            y = _dot(inner.astype(BF16), vc)
            if c > 0:
                y = y + _dot(qc, state.astype(BF16)) * xi
            kz_t = (kc.astype(F32) * zeta).T.astype(BF16)
            state = state * g_chunk + _dot(kz_t, vc)
            mu = jnp.mean(y, axis=-1, keepdims=True)
            yc = y - mu
            var = jnp.mean(yc * yc, axis=-1, keepdims=True)
            r = yc * lax.rsqrt(var + GN_EPS) * gn_w + gn_b
            g = g_ref[rows, :].astype(F32)
            r_ref[rows, :] = (g * jax.nn.sigmoid(g) * r).astype(BF16)
            yield

    retention = retention_stage()
    for _ in score_stage(0):
        pass
    for i in range(nb):
        stages = [value_stage(i)] + ([score_stage(i + 1)] if i + 1 < nb else [])
        first_pass = True
        while stages:
            for stage in list(stages):
                if next(stage, "done") == "done":
                    stages.remove(stage)
            if first_pass:
                next(retention, None)
                first_pass = False
    for _ in retention:
        pass


def _mixer(proj3, log_g, gn_w, gn_b, weights_f32):
    batch, seq, _ = proj3.shape
    nb = seq // MOBA_BLOCK
    assert nb <= HEAD_DIM, "block-selection rows must fit in the spare contraction rows"
    n_steps = batch * N_HEADS

    def col(group):
        return pl.BlockSpec((None, seq, HEAD_DIM), lambda b, h, lg: (b, 0, group * N_HEADS + h))

    head_vec = pl.BlockSpec((1, HEAD_DIM), lambda b, h, lg: (0, h))
    head_out = pl.BlockSpec((None, seq, HEAD_DIM), lambda b, h, lg: (b, 0, h))

    cast_in, cast_out, cast_shapes = [], [], []
    for w, col_chunk in weights_f32:
        n_rows, n_cols = w.shape
        rows = n_rows // n_steps
        assert rows * n_steps == n_rows and rows % BF16_SUBLANES == 0
        cast_in.append(pl.BlockSpec((rows, n_cols), lambda b, h, lg: (b * N_HEADS + h, 0)))
        if col_chunk is None:
            cast_out.append(cast_in[-1])
            cast_shapes.append(jax.ShapeDtypeStruct(w.shape, BF16))
        else:
            n_chunks = n_cols // col_chunk
            cast_out.append(pl.BlockSpec((n_chunks, rows, col_chunk),
                                         lambda b, h, lg: (0, b * N_HEADS + h, 0)))
            cast_shapes.append(jax.ShapeDtypeStruct((n_chunks, n_rows, col_chunk), BF16))

    head_shape = jax.ShapeDtypeStruct((batch, seq, GROUP_WIDTH), BF16)
    outs = pl.pallas_call(
        _mixer_kernel,
        grid_spec=pltpu.PrefetchScalarGridSpec(
            num_scalar_prefetch=1,
            grid=(batch, N_HEADS),
            in_specs=[col(0), col(1), col(2), col(3), col(4), col(5), col(6), head_vec, head_vec]
            + cast_in,
            out_specs=[head_out, head_out] + cast_out,
            scratch_shapes=[
                pltpu.VMEM((seq, 2 * HEAD_DIM), BF16),
                pltpu.VMEM((HEAD_DIM, seq), BF16),
                pltpu.VMEM((nb, HEAD_DIM), F32),
                pltpu.VMEM((2, 2 * HEAD_DIM, MOBA_BLOCK), BF16),
                pltpu.VMEM((2, seq, MOBA_BLOCK), F32),
                pltpu.VMEM((2, seq, MOBA_BLOCK), BF16),
            ],
        ),
        out_shape=[head_shape, head_shape] + cast_shapes,
        compiler_params=pltpu.CompilerParams(
            dimension_semantics=("arbitrary", "arbitrary"),
            vmem_limit_bytes=V7X_VMEM_LIMIT_BYTES),
        name="mixer",
    )(log_g, *([proj3] * 7), gn_w, gn_b, *[w for w, _ in weights_f32])
    return outs[0], outs[1], outs[2:]


def _outproj_kernel(a_ref, r_ref, w_ref, x_ref, post_ref, pre_ref, x1_ref, hn_ref):
    for rows in _sub_tiles(x_ref.shape[0]):
        mix = (_dot(a_ref[rows, :], w_ref[:GROUP_WIDTH, :])
               + _dot(r_ref[rows, :], w_ref[GROUP_WIDTH:, :]))
        x1 = x_ref[rows, :] + mix * _rms_scale(mix) * post_ref[...]
        x1_ref[rows, :] = x1
        hn_ref[rows, :] = (x1 * _rms_scale(x1) * pre_ref[...]).astype(BF16)


def _outproj(a2, r2, w_out_bf16, x2, norm_post, norm_pre, tm):
    tokens, d_model = x2.shape
    row_tile = lambda width: pl.BlockSpec((tm, width), lambda i: (i, 0))
    whole = lambda shape: pl.BlockSpec(shape, lambda i: (0, 0))
    return pl.pallas_call(
        _outproj_kernel,
        grid=(tokens // tm,),
        in_specs=[row_tile(GROUP_WIDTH), row_tile(GROUP_WIDTH), whole(w_out_bf16.shape),
                  row_tile(d_model), whole((1, d_model)), whole((1, d_model))],
        out_specs=[row_tile(d_model), row_tile(d_model)],
        out_shape=[jax.ShapeDtypeStruct((tokens, d_model), F32),
                   jax.ShapeDtypeStruct((tokens, d_model), BF16)],
        compiler_params=pltpu.CompilerParams(
            dimension_semantics=("arbitrary",),
            vmem_limit_bytes=V7X_VMEM_LIMIT_BYTES),
        name="outproj",
    )(a2, r2, w_out_bf16, x2, norm_post, norm_pre)


def _ffn_kernel(hn_ref, wg_ref, wv_ref, cw_ref, cb_ref, wd_ref, x1c_ref, post_ref,
                o_ref, x1_ref, carry_ref, *, tiles_per_seq):
    i = pl.program_id(0)
    j = pl.program_id(1)
    last = pl.num_programs(1) - 1
    tf = wg_ref.shape[1]
    n_x1, _, x1_cols = x1_ref.shape

    @pl.when(i % tiles_per_seq == 0)
    def _():
        carry_ref[j] = jnp.zeros((F32_SUBLANES, tf), F32)

    @pl.when(j < n_x1)
    def _():
        x1_ref[j] = x1c_ref[...]

    def body(first_chunk, last_chunk):
        cw = cw_ref[...]
        cb = cb_ref[...]
        history = carry_ref[j]
        row = lax.broadcasted_iota(jnp.int32, (SUB_ROWS, tf), 0)
        tiles = _sub_tiles(hn_ref.shape[0])
        ug, uv, hidden = {}, {}, {}

        def up(s):
            hn = hn_ref[tiles[s], :]
            ug[s] = _dot(hn, wg_ref[...])
            uv[s] = _dot(hn, wv_ref[...])

        def conv(s):
            if s == 0:
                prev2 = history[F32_SUBLANES - 2:F32_SUBLANES - 1, :]
                prev1 = history[F32_SUBLANES - 1:, :]
            else:
                prev2, prev1 = ug[s - 1][SUB_ROWS - 2:SUB_ROWS - 1, :], ug[s - 1][SUB_ROWS - 1:, :]
            g = ug[s]
            back1 = jnp.where(row == 0, prev1, pltpu.roll(g, 1, axis=0))
            back2 = jnp.where(row == 0, prev2,
                              jnp.where(row == 1, prev1, pltpu.roll(g, 2, axis=0)))
            c = cw[0:1, :] * back2 + cw[1:2, :] * back1 + cw[2:3, :] * g + cb
            hidden[s] = (c * jax.nn.sigmoid(c) * uv.pop(s)).astype(BF16)

        def down(s):
            rows = tiles[s]
            part = _dot(hidden.pop(s), wd_ref[...])
            ff = part if first_chunk else o_ref[rows, :] + part
            if last_chunk:
                normed = ff * _rms_scale(ff) * post_ref[...]
                for c in range(n_x1):
                    cols = slice(c * x1_cols, (c + 1) * x1_cols)
                    o_ref[rows, cols] = x1_ref[c, rows, :] + normed[:, cols]
            else:
                o_ref[rows, :] = ff

        n_sub = len(tiles)
        up(0)
        for s in range(n_sub):
            if s + 1 < n_sub:
                up(s + 1)
            if s >= 1:
                down(s - 1)
            conv(s)
        down(n_sub - 1)
        carry_ref[j] = ug[n_sub - 1][SUB_ROWS - F32_SUBLANES:, :]

    pl.when(j == 0)(lambda: body(True, False))
    pl.when((j > 0) & (j < last))(lambda: body(False, False))
    pl.when(j == last)(lambda: body(False, True))


def _ffn(hn2, w_up_bf16, conv_w, conv_b, w_down_bf16, x1, norm_post, seq, tm, tf):
    tokens, d_model = x1.shape
    d_ff = w_down_bf16.shape[0]
    n_chunks = d_ff // tf
    n_x1 = d_model // X1_SLAB_COLS
    assert n_chunks >= 2 and n_chunks >= n_x1
    row_tile = pl.BlockSpec((tm, d_model), lambda i, j: (i, 0))
    return pl.pallas_call(
        functools.partial(_ffn_kernel, tiles_per_seq=seq // tm),
        grid=(tokens // tm, n_chunks),
        in_specs=[
            row_tile,
            pl.BlockSpec((None, d_model, tf), lambda i, j: (j, 0, 0)),
            pl.BlockSpec((None, d_model, tf), lambda i, j: (n_chunks + j, 0, 0)),
            pl.BlockSpec((CONV_WIDTH, tf), lambda i, j: (0, j)),
            pl.BlockSpec((1, tf), lambda i, j: (0, j)),
            pl.BlockSpec((tf, d_model), lambda i, j: (j, 0)),
            pl.BlockSpec((tm, X1_SLAB_COLS), lambda i, j: (i, jnp.minimum(j, n_x1 - 1))),
            pl.BlockSpec((1, d_model), lambda i, j: (0, 0)),
        ],
        out_specs=row_tile,
        out_shape=jax.ShapeDtypeStruct((tokens, d_model), F32),
        scratch_shapes=[pltpu.VMEM((n_x1, tm, X1_SLAB_COLS), F32),
                        pltpu.VMEM((n_chunks, F32_SUBLANES, tf), F32)],
        compiler_params=pltpu.CompilerParams(
            dimension_semantics=("arbitrary", "arbitrary"),
            vmem_limit_bytes=V7X_VMEM_LIMIT_BYTES),
        name="ffn",
    )(hn2, w_up_bf16, w_up_bf16, conv_w, conv_b, w_down_bf16, x1, norm_post)


def _rope_tables(seq):
    pos = np.arange(seq, dtype=np.float32)
    inv = (1.0 / (ROPE_THETA ** (np.arange(0, HEAD_DIM, 2, dtype=np.float32) / HEAD_DIM))).astype(np.float32)
    ang = pos[:, None] * inv[None, :]
    cos, sin = np.cos(ang), np.sin(ang)
    cos_full = np.concatenate([cos, cos], axis=-1)
    sin_signed = np.concatenate([-sin, sin], axis=-1)
    q_scale = np.float32(MOBA_Q_SCALE)
    tab_a = np.stack([cos_full * q_scale, cos_full])
    tab_b = np.stack([sin_signed * q_scale, sin_signed])
    return jnp.asarray(tab_a, F32), jnp.asarray(tab_b, F32)


def _retention_log_decay():
    heads = np.arange(N_HEADS, dtype=np.float32)
    return jnp.asarray(np.log(np.float32(1.0) - np.float32(2.0) ** (-5.0 - heads)), F32)


def _tile(n, preferred):
    for t in preferred:
        if n % t == 0:
            return t
    raise ValueError(f"no supported tile for extent {n}")


def kernel(x, norm_mix_pre, w_in, ret_gn_w, ret_gn_b, w_out, norm_mix_post, norm_ffn_pre, w_up, conv_w, conv_b, w_down, norm_ffn_post):
    batch, seq, d_model = x.shape
    depth = w_in.shape[0]
    assert seq % MOBA_BLOCK == 0 and seq % RET_CHUNK == 0
    assert w_in.shape[2] == 7 * GROUP_WIDTH and w_out.shape[1] == 2 * GROUP_WIDTH
    d_ff = w_down.shape[1]
    tokens = batch * seq
    tm_proj = _tile(seq, (256,))
    tm_out = _tile(seq, (512, 256))
    tm_ffn = _tile(seq, (1024, 512, 256))
    tf = _tile(d_ff, (512, 256, 128))

    tab_a, tab_b = _rope_tables(seq)
    log_g = _retention_log_decay()

    x2 = x.reshape(tokens, d_model)
    for layer in range(depth):
        proj = _inproj(x2, norm_mix_pre[layer][None, :], w_in[layer].astype(BF16),
                       tab_a, tab_b, seq, tm_proj)
        proj3 = proj.reshape(batch, seq, -1)
        a, r, (w_out_bf16, w_up_bf16, w_down_bf16) = _mixer(
            proj3, log_g, ret_gn_w[layer][None, :], ret_gn_b[layer][None, :],
            ((w_out[layer], None), (w_up[layer], tf), (w_down[layer], None)))
        x1, hn2 = _outproj(a.reshape(tokens, GROUP_WIDTH), r.reshape(tokens, GROUP_WIDTH),
                           w_out_bf16, x2,
                           norm_mix_post[layer][None, :], norm_ffn_pre[layer][None, :], tm_out)
        x2 = _ffn(hn2, w_up_bf16, conv_w[layer], conv_b[layer][None, :],
                  w_down_bf16, x1, norm_ffn_post[layer][None, :], seq, tm_ffn, tf)
    return x2.reshape(batch, seq, d_model)
```

```python
import functools

import jax
import jax.numpy as jnp
import numpy as np
from jax import lax
from jax.experimental import pallas as pl
from jax.experimental.pallas import tpu as pltpu

HEAD_DIM = 128
N_HEADS = 8
GROUP_WIDTH = N_HEADS * HEAD_DIM
MOBA_BLOCK = 256
MOBA_TOPK = 3
RET_CHUNK = 256
ROPE_THETA = 10000.0
CONV_WIDTH = 3
NORM_EPS = 1e-6
GN_EPS = 1e-5
ATTN_SCALE = HEAD_DIM ** -0.5
MASKED_SCORE = -1e30
MOBA_Q_SCALE = ATTN_SCALE * 1.4426950408889634

V7X_VMEM_LIMIT_BYTES = 60 * 1024 * 1024
SUB_ROWS = 256
X1_SLAB_COLS = 256
F32_SUBLANES = 8
BF16_SUBLANES = 16

F32 = jnp.float32
BF16 = jnp.bfloat16


def _dot(a, b):
    return jnp.dot(a, b, preferred_element_type=F32)


def _dot_nt(a, b):
    return lax.dot_general(a, b, (((1,), (1,)), ((), ())), preferred_element_type=F32)


def _rms_scale(v):
    return lax.rsqrt(jnp.mean(v * v, axis=-1, keepdims=True) + NORM_EPS)


def _sub_tiles(rows):
    return [slice(r, r + SUB_ROWS) for r in range(0, rows, SUB_ROWS)]


ROTARY_SCALED, ROTARY, PLAIN = 0, 1, 2
INPROJ_GROUP_KIND = (ROTARY_SCALED, ROTARY, PLAIN, ROTARY, ROTARY, PLAIN, PLAIN)


def _inproj_kernel(x_ref, nw_ref, w_ref, a_ref, b_ref, o_ref):
    x = x_ref[...]
    hn = (x * _rms_scale(x) * nw_ref[...]).astype(BF16)
    acc = {}

    def project(g):
        acc[g] = _dot(hn, w_ref[:, g * GROUP_WIDTH:(g + 1) * GROUP_WIDTH])

    def finish(g):
        kind = INPROJ_GROUP_KIND[g]
        t_all = acc.pop(g)
        if kind == PLAIN:
            o_ref[:, g * GROUP_WIDTH:(g + 1) * GROUP_WIDTH] = t_all.astype(BF16)
            return
        a = a_ref[kind]
        b = b_ref[kind]
        for h in range(N_HEADS):
            t = t_all[:, h * HEAD_DIM:(h + 1) * HEAD_DIM]
            rot = pltpu.roll(t, HEAD_DIM // 2, axis=1)
            cols = slice(g * GROUP_WIDTH + h * HEAD_DIM, g * GROUP_WIDTH + (h + 1) * HEAD_DIM)
            o_ref[:, cols] = (t * a + rot * b).astype(BF16)

    n_groups = len(INPROJ_GROUP_KIND)
    project(0)
    for g in range(n_groups):
        if g + 1 < n_groups:
            project(g + 1)
        finish(g)


def _inproj(x2, norm_w, w_in_bf16, tab_a, tab_b, seq, tm):
    tokens, d_model = x2.shape
    in_cols = w_in_bf16.shape[1]
    assert in_cols == len(INPROJ_GROUP_KIND) * GROUP_WIDTH
    tiles_per_seq = seq // tm
    table = pl.BlockSpec((tab_a.shape[0], tm, HEAD_DIM), lambda i: (0, i % tiles_per_seq, 0))
    return pl.pallas_call(
        _inproj_kernel,
        grid=(tokens // tm,),
        in_specs=[
            pl.BlockSpec((tm, d_model), lambda i: (i, 0)),
            pl.BlockSpec((1, d_model), lambda i: (0, 0)),
            pl.BlockSpec((d_model, in_cols), lambda i: (0, 0), pipeline_mode=pl.Buffered(1)),
            table,
            table,
        ],
        out_specs=pl.BlockSpec((tm, in_cols), lambda i: (i, 0)),
        out_shape=jax.ShapeDtypeStruct((tokens, in_cols), BF16),
        compiler_params=pltpu.CompilerParams(
            dimension_semantics=("arbitrary",),
            vmem_limit_bytes=V7X_VMEM_LIMIT_BYTES),
        name="inproj",
    )(x2, norm_w, w_in_bf16, tab_a, tab_b)


def _mixer_kernel(logg_ref, q_ref, k_ref, v_ref, qr_ref, kr_ref, vr_ref, g_ref, gw_ref, gb_ref, *rest):
    n_cast = (len(rest) - 8) // 2
    cast_in, a_ref, r_ref = rest[:n_cast], rest[n_cast], rest[n_cast + 1]
    cast_out = rest[n_cast + 2:2 * n_cast + 2]
    kaug_ref, vt_ref, kmean_ref, qt_ref, s_ref, p_ref = rest[2 * n_cast + 2:]
    seq = q_ref.shape[0]
    nb = seq // MOBA_BLOCK

    for src, dst in zip(cast_in, cast_out):
        if len(dst.shape) == 2:
            dst[...] = src[...].astype(BF16)
        else:
            width = dst.shape[2]
            for c in range(dst.shape[0]):
                dst[c] = src[:, c * width:(c + 1) * width].astype(BF16)

    lane = lax.broadcasted_iota(jnp.int32, (MOBA_BLOCK, HEAD_DIM), 1)
    for jb in range(nb):
        rows = slice(jb * MOBA_BLOCK, (jb + 1) * MOBA_BLOCK)
        kb = k_ref[rows, :]
        kaug_ref[rows, :HEAD_DIM] = kb
        kaug_ref[rows, HEAD_DIM:] = jnp.where(lane == jb, 1.0, 0.0).astype(BF16)
        vt_ref[:, rows] = v_ref[rows, :].astype(F32).T.astype(BF16)
        kmean_ref[jb:jb + 1, :] = jnp.mean(kb.astype(F32), axis=0, keepdims=True)
    for slot in range(2):
        qt_ref[slot, HEAD_DIM:, :] = jnp.zeros((HEAD_DIM, MOBA_BLOCK), BF16)
    kmean = kmean_ref[...].astype(BF16)

    blk_idx = lax.broadcasted_iota(jnp.int32, (nb, MOBA_BLOCK), 0)
    key_pos = lax.broadcasted_iota(jnp.int32, (MOBA_BLOCK, MOBA_BLOCK), 0)
    qry_pos = lax.broadcasted_iota(jnp.int32, (MOBA_BLOCK, MOBA_BLOCK), 1)
    causal = key_pos <= qry_pos

    col_max = {}

    def score_stage(i):
        slot = i % 2
        qt = q_ref[i * MOBA_BLOCK:(i + 1) * MOBA_BLOCK, :].astype(F32).T.astype(BF16)
        past = blk_idx < i
        gate = jnp.where(past, _dot(kmean, qt), -jnp.inf)
        beaten_by = jnp.zeros((nb, MOBA_BLOCK), jnp.int32)
        for jp in range(i):
            row = gate[jp:jp + 1, :]
            beats = (row > gate) | ((row == gate) & (jp < blk_idx))
            beaten_by = beaten_by + beats.astype(jnp.int32)
        visible = (past & (beaten_by < MOBA_TOPK)) | (blk_idx == i)
        qt_ref[slot, :HEAD_DIM, :] = qt
        qt_ref[slot, HEAD_DIM:HEAD_DIM + nb, :] = jnp.where(visible, 0.0, MASKED_SCORE).astype(BF16)
        qt_aug = qt_ref[slot]
        yield
        m = None
        for t in range(i + 1):
            k_rows = slice(t * MOBA_BLOCK, (t + 1) * MOBA_BLOCK)
            s = _dot(kaug_ref[k_rows, :], qt_aug)
            if t == i:
                s = jnp.where(causal, s, MASKED_SCORE)
            s_ref[slot, k_rows, :] = s
            tile_max = jnp.max(s, axis=0, keepdims=True)
            m = tile_max if m is None else jnp.maximum(m, tile_max)
            yield
        col_max[i] = m

    def value_stage(i):
        slot = i % 2
        m = col_max.pop(i)
        l = None
        for t in range(i + 1):
            k_rows = slice(t * MOBA_BLOCK, (t + 1) * MOBA_BLOCK)
            p = jnp.exp2(s_ref[slot, k_rows, :] - m)
            tile_sum = jnp.sum(p, axis=0, keepdims=True)
            l = tile_sum if l is None else l + tile_sum
            p_ref[slot, k_rows, :] = p.astype(BF16)
            yield
        n_keys = (i + 1) * MOBA_BLOCK
        out_t = _dot(vt_ref[:, :n_keys], p_ref[slot, :n_keys, :]) / l
        a_ref[i * MOBA_BLOCK:(i + 1) * MOBA_BLOCK, :] = out_t.T.astype(BF16)
        yield

    def retention_stage():
        log_g = logg_ref[pl.program_id(1)]
        row = lax.broadcasted_iota(jnp.int32, (RET_CHUNK, RET_CHUNK), 0)
        colm = lax.broadcasted_iota(jnp.int32, (RET_CHUNK, RET_CHUNK), 1)
        rel = (row - colm).astype(F32)
        decay = jnp.where(rel >= 0, jnp.exp(log_g * jnp.maximum(rel, 0.0)), 0.0) * ATTN_SCALE
        pos = lax.broadcasted_iota(jnp.int32, (RET_CHUNK, HEAD_DIM), 0).astype(F32)
        xi = jnp.exp(log_g * (pos + 1.0))
        zeta = jnp.exp(log_g * (RET_CHUNK - 1.0 - pos)) * ATTN_SCALE
        g_chunk = jnp.exp(log_g * RET_CHUNK)
        gn_w = gw_ref[...]
        gn_b = gb_ref[...]
        state = jnp.zeros((HEAD_DIM, HEAD_DIM), F32)
        for c in range(seq // RET_CHUNK):
            rows = slice(c * RET_CHUNK, (c + 1) * RET_CHUNK)
            qc = qr_ref[rows, :]
            kc = kr_ref[rows, :]
            vc = vr_ref[rows, :]
            inner = _dot_nt(qc, kc) * decay
            y = _dot(inner.astype(BF16), vc)
            if c > 0:
                y = y + _dot(qc, state.astype(BF16)) * xi
            kz_t = (kc.astype(F32) * zeta).T.astype(BF16)
            state = state * g_chunk + _dot(kz_t, vc)
            mu = jnp.mean(y, axis=-1, keepdims=True)
            yc = y - mu
            var = jnp.mean(yc * yc, axis=-1, keepdims=True)
            r = yc * lax.rsqrt(var + GN_EPS) * gn_w + gn_b
            g = g_ref[rows, :].astype(F32)
            r_ref[rows, :] = (g * jax.nn.sigmoid(g) * r).astype(BF16)
            yield

    retention = retention_stage()
    for _ in score_stage(0):
        pass
    for i in range(nb):
        stages = [value_stage(i)] + ([score_stage(i + 1)] if i + 1 < nb else [])
        first_pass = True
        while stages:
            for stage in list(stages):
                if next(stage, "done") == "done":
                    stages.remove(stage)
            if first_pass:
                next(retention, None)
                first_pass = False
    for _ in retention:
        pass


def _mixer(proj3, log_g, gn_w, gn_b, weights_f32):
    batch, seq, _ = proj3.shape
    nb = seq // MOBA_BLOCK
    assert nb <= HEAD_DIM, "block-selection rows must fit in the spare contraction rows"
    n_steps = batch * N_HEADS

    def col(group):
        return pl.BlockSpec((None, seq, HEAD_DIM), lambda b, h, lg: (b, 0, group * N_HEADS + h))

    head_vec = pl.BlockSpec((1, HEAD_DIM), lambda b, h, lg: (0, h))
    head_out = pl.BlockSpec((None, seq, HEAD_DIM), lambda b, h, lg: (b, 0, h))

    cast_in, cast_out, cast_shapes = [], [], []
    for w, col_chunk in weights_f32:
        n_rows, n_cols = w.shape
        rows = n_rows // n_steps
        assert rows * n_steps == n_rows and rows % BF16_SUBLANES == 0
        cast_in.append(pl.BlockSpec((rows, n_cols), lambda b, h, lg: (b * N_HEADS + h, 0)))
        if col_chunk is None:
            cast_out.append(cast_in[-1])
            cast_shapes.append(jax.ShapeDtypeStruct(w.shape, BF16))
        else:
            n_chunks = n_cols // col_chunk
            cast_out.append(pl.BlockSpec((n_chunks, rows, col_chunk),
                                         lambda b, h, lg: (0, b * N_HEADS + h, 0)))
            cast_shapes.append(jax.ShapeDtypeStruct((n_chunks, n_rows, col_chunk), BF16))

    head_shape = jax.ShapeDtypeStruct((batch, seq, GROUP_WIDTH), BF16)
    outs = pl.pallas_call(
        _mixer_kernel,
        grid_spec=pltpu.PrefetchScalarGridSpec(
            num_scalar_prefetch=1,
            grid=(batch, N_HEADS),
            in_specs=[col(0), col(1), col(2), col(3), col(4), col(5), col(6), head_vec, head_vec]
            + cast_in,
            out_specs=[head_out, head_out] + cast_out,
            scratch_shapes=[
                pltpu.VMEM((seq, 2 * HEAD_DIM), BF16),
                pltpu.VMEM((HEAD_DIM, seq), BF16),
                pltpu.VMEM((nb, HEAD_DIM), F32),
                pltpu.VMEM((2, 2 * HEAD_DIM, MOBA_BLOCK), BF16),
                pltpu.VMEM((2, seq, MOBA_BLOCK), F32),
                pltpu.VMEM((2, seq, MOBA_BLOCK), BF16),
            ],
        ),
        out_shape=[head_shape, head_shape] + cast_shapes,
        compiler_params=pltpu.CompilerParams(
            dimension_semantics=("arbitrary", "arbitrary"),
            vmem_limit_bytes=V7X_VMEM_LIMIT_BYTES),
        name="mixer",
    )(log_g, *([proj3] * 7), gn_w, gn_b, *[w for w, _ in weights_f32])
    return outs[0], outs[1], outs[2:]


def _outproj_kernel(a_ref, r_ref, w_ref, x_ref, post_ref, pre_ref, x1_ref, hn_ref):
    for rows in _sub_tiles(x_ref.shape[0]):
        mix = (_dot(a_ref[rows, :], w_ref[:GROUP_WIDTH, :])
               + _dot(r_ref[rows, :], w_ref[GROUP_WIDTH:, :]))
        x1 = x_ref[rows, :] + mix * _rms_scale(mix) * post_ref[...]
        x1_ref[rows, :] = x1
        hn_ref[rows, :] = (x1 * _rms_scale(x1) * pre_ref[...]).astype(BF16)


def _outproj(a2, r2, w_out_bf16, x2, norm_post, norm_pre, tm):
    tokens, d_model = x2.shape
    row_tile = lambda width: pl.BlockSpec((tm, width), lambda i: (i, 0))
    whole = lambda shape: pl.BlockSpec(shape, lambda i: (0, 0))
    return pl.pallas_call(
        _outproj_kernel,
        grid=(tokens // tm,),
        in_specs=[row_tile(GROUP_WIDTH), row_tile(GROUP_WIDTH), whole(w_out_bf16.shape),
                  row_tile(d_model), whole((1, d_model)), whole((1, d_model))],
        out_specs=[row_tile(d_model), row_tile(d_model)],
        out_shape=[jax.ShapeDtypeStruct((tokens, d_model), F32),
                   jax.ShapeDtypeStruct((tokens, d_model), BF16)],
        compiler_params=pltpu.CompilerParams(
            dimension_semantics=("arbitrary",),
            vmem_limit_bytes=V7X_VMEM_LIMIT_BYTES),
        name="outproj",
    )(a2, r2, w_out_bf16, x2, norm_post, norm_pre)


def _ffn_kernel(hn_ref, wg_ref, wv_ref, cw_ref, cb_ref, wd_ref, x1c_ref, post_ref,
                o_ref, x1_ref, carry_ref, *, tiles_per_seq):
    i = pl.program_id(0)
    j = pl.program_id(1)
    last = pl.num_programs(1) - 1
    tf = wg_ref.shape[1]
    n_x1, _, x1_cols = x1_ref.shape

    @pl.when(i % tiles_per_seq == 0)
    def _():
        carry_ref[j] = jnp.zeros((F32_SUBLANES, tf), F32)

    @pl.when(j < n_x1)
    def _():
        x1_ref[j] = x1c_ref[...]

    def body(first_chunk, last_chunk):
        cw = cw_ref[...]
        cb = cb_ref[...]
        history = carry_ref[j]
        row = lax.broadcasted_iota(jnp.int32, (SUB_ROWS, tf), 0)
        tiles = _sub_tiles(hn_ref.shape[0])
        ug, uv, hidden = {}, {}, {}

        def up(s):
            hn = hn_ref[tiles[s], :]
            ug[s] = _dot(hn, wg_ref[...])
            uv[s] = _dot(hn, wv_ref[...])

        def conv(s):
            if s == 0:
                prev2 = history[F32_SUBLANES - 2:F32_SUBLANES - 1, :]
                prev1 = history[F32_SUBLANES - 1:, :]
            else:
                prev2, prev1 = ug[s - 1][SUB_ROWS - 2:SUB_ROWS - 1, :], ug[s - 1][SUB_ROWS - 1:, :]
            g = ug[s]
            back1 = jnp.where(row == 0, prev1, pltpu.roll(g, 1, axis=0))
            back2 = jnp.where(row == 0, prev2,
                              jnp.where(row == 1, prev1, pltpu.roll(g, 2, axis=0)))
            c = cw[0:1, :] * back2 + cw[1:2, :] * back1 + cw[2:3, :] * g + cb
            hidden[s] = (c * jax.nn.sigmoid(c) * uv.pop(s)).astype(BF16)

        def down(s):
            rows = tiles[s]
            part = _dot(hidden.pop(s), wd_ref[...])
            ff = part if first_chunk else o_ref[rows, :] + part
            if last_chunk:
                normed = ff * _rms_scale(ff) * post_ref[...]
                for c in range(n_x1):
                    cols = slice(c * x1_cols, (c + 1) * x1_cols)
                    o_ref[rows, cols] = x1_ref[c, rows, :] + normed[:, cols]
            else:
                o_ref[rows, :] = ff

        n_sub = len(tiles)
        up(0)
        for s in range(n_sub):
            if s + 1 < n_sub:
                up(s + 1)
            if s >= 1:
                down(s - 1)
            conv(s)
        down(n_sub - 1)
        carry_ref[j] = ug[n_sub - 1][SUB_ROWS - F32_SUBLANES:, :]

    pl.when(j == 0)(lambda: body(True, False))
    pl.when((j > 0) & (j < last))(lambda: body(False, False))
    pl.when(j == last)(lambda: body(False, True))


def _ffn(hn2, w_up_bf16, conv_w, conv_b, w_down_bf16, x1, norm_post, seq, tm, tf):
    tokens, d_model = x1.shape
    d_ff = w_down_bf16.shape[0]
    n_chunks = d_ff // tf
    n_x1 = d_model // X1_SLAB_COLS
    assert n_chunks >= 2 and n_chunks >= n_x1
    row_tile = pl.BlockSpec((tm, d_model), lambda i, j: (i, 0))
    return pl.pallas_call(
        functools.partial(_ffn_kernel, tiles_per_seq=seq // tm),
        grid=(tokens // tm, n_chunks),
        in_specs=[
            row_tile,
            pl.BlockSpec((None, d_model, tf), lambda i, j: (j, 0, 0)),
            pl.BlockSpec((None, d_model, tf), lambda i, j: (n_chunks + j, 0, 0)),
            pl.BlockSpec((CONV_WIDTH, tf), lambda i, j: (0, j)),
            pl.BlockSpec((1, tf), lambda i, j: (0, j)),
            pl.BlockSpec((tf, d_model), lambda i, j: (j, 0)),
            pl.BlockSpec((tm, X1_SLAB_COLS), lambda i, j: (i, jnp.minimum(j, n_x1 - 1))),
            pl.BlockSpec((1, d_model), lambda i, j: (0, 0)),
        ],
        out_specs=row_tile,
        out_shape=jax.ShapeDtypeStruct((tokens, d_model), F32),
        scratch_shapes=[pltpu.VMEM((n_x1, tm, X1_SLAB_COLS), F32),
                        pltpu.VMEM((n_chunks, F32_SUBLANES, tf), F32)],
        compiler_params=pltpu.CompilerParams(
            dimension_semantics=("arbitrary", "arbitrary"),
            vmem_limit_bytes=V7X_VMEM_LIMIT_BYTES),
        name="ffn",
    )(hn2, w_up_bf16, w_up_bf16, conv_w, conv_b, w_down_bf16, x1, norm_post)


def _rope_tables(seq):
    pos = np.arange(seq, dtype=np.float32)
    inv = (1.0 / (ROPE_THETA ** (np.arange(0, HEAD_DIM, 2, dtype=np.float32) / HEAD_DIM))).astype(np.float32)
    ang = pos[:, None] * inv[None, :]
    cos, sin = np.cos(ang), np.sin(ang)
    cos_full = np.concatenate([cos, cos], axis=-1)
    sin_signed = np.concatenate([-sin, sin], axis=-1)
    q_scale = np.float32(MOBA_Q_SCALE)
    tab_a = np.stack([cos_full * q_scale, cos_full])
    tab_b = np.stack([sin_signed * q_scale, sin_signed])
    return jnp.asarray(tab_a, F32), jnp.asarray(tab_b, F32)


def _retention_log_decay():
    heads = np.arange(N_HEADS, dtype=np.float32)
    return jnp.asarray(np.log(np.float32(1.0) - np.float32(2.0) ** (-5.0 - heads)), F32)


def _tile(n, preferred):
    for t in preferred:
        if n % t == 0:
            return t
    raise ValueError(f"no supported tile for extent {n}")


def kernel(x, norm_mix_pre, w_in, ret_gn_w, ret_gn_b, w_out, norm_mix_post, norm_ffn_pre, w_up, conv_w, conv_b, w_down, norm_ffn_post):
    batch, seq, d_model = x.shape
    depth = w_in.shape[0]
    assert seq % MOBA_BLOCK == 0 and seq % RET_CHUNK == 0
    assert w_in.shape[2] == 7 * GROUP_WIDTH and w_out.shape[1] == 2 * GROUP_WIDTH
    d_ff = w_down.shape[1]
    tokens = batch * seq
    tm_proj = _tile(seq, (512, 256))
    tm_out = _tile(seq, (512, 256))
    tm_ffn = _tile(seq, (1024, 512, 256))
    tf = _tile(d_ff, (512, 256, 128))

    tab_a, tab_b = _rope_tables(seq)
    log_g = _retention_log_decay()

    x2 = x.reshape(tokens, d_model)
    for layer in range(depth):
        proj = _inproj(x2, norm_mix_pre[layer][None, :], w_in[layer].astype(BF16),
                       tab_a, tab_b, seq, tm_proj)
        proj3 = proj.reshape(batch, seq, -1)
        a, r, (w_out_bf16, w_up_bf16, w_down_bf16) = _mixer(
            proj3, log_g, ret_gn_w[layer][None, :], ret_gn_b[layer][None, :],
            ((w_out[layer], None), (w_up[layer], tf), (w_down[layer], None)))
        x1, hn2 = _outproj(a.reshape(tokens, GROUP_WIDTH), r.reshape(tokens, GROUP_WIDTH),
                           w_out_bf16, x2,
                           norm_mix_post[layer][None, :], norm_ffn_pre[layer][None, :], tm_out)
        x2 = _ffn(hn2, w_up_bf16, conv_w[layer], conv_b[layer][None, :],
                  w_down_bf16, x1, norm_ffn_post[layer][None, :], seq, tm_ffn, tf)
    return x2.reshape(batch, seq, d_model)
```
